```python
import math
import jax, jax.numpy as jnp
from jax import lax
import numpy as np

D_MODEL = 1024
BATCH = 32
SEQ = 2048
DEPTH = 2

EPS = 1e-6
N_BRANCH = 3
BRANCH_W = D_MODEL // 2
SSM_GROUP = 16
SSM_GROUPS = BRANCH_W // SSM_GROUP
SSM_STATE = 64
DT_MIN = 1e-3
DT_MAX = 1e-1
HEAD_DIM = 64
ATT_HEADS = BRANCH_W // HEAD_DIM
DILATED_PATTERNS = ((128, 1), (512, 4), (2048, 16))
N_PAT = len(DILATED_PATTERNS)
ATT_BLOCK = 128
ATT_SCALE = HEAD_DIM ** -0.5
CONV_WIDTH = 31
D_FF = -(-(8 * D_MODEL) // (3 * 256)) * 256

COL_U = BRANCH_W
COL_Q = N_PAT * BRANCH_W
COL_KV = BRANCH_W
COL_CONV = 2 * BRANCH_W
COL_GATE = N_BRANCH * D_MODEL
SPLIT_POINTS = (COL_U,
                COL_U + COL_Q,
                COL_U + COL_Q + COL_KV,
                COL_U + COL_Q + 2 * COL_KV,
                COL_U + COL_Q + 2 * COL_KV + COL_CONV)
IN_COLS = SPLIT_POINTS[-1] + COL_GATE

kernel_name = 'hybrid_s5_dilated_attn_conformer_conv_block'


def rms_norm(x, g):
    xf = x.astype(jnp.float32)
    y = xf * lax.rsqrt(jnp.mean(xf * xf, axis=-1, keepdims=True) + EPS)
    return (y * g.astype(jnp.float32)).astype(x.dtype)


def _complex_affine_combine(e1, e2):
    a1r, a1i, b1r, b1i = e1
    a2r, a2i, b2r, b2i = e2
    return (a1r * a2r - a1i * a2i,
            a1r * a2i + a1i * a2r,
            a2r * b1r - a2i * b1i + b2r,
            a2r * b1i + a2i * b1r + b2i)


def s5_branch(u, lam_re, lam_im, log_dt, b_re, b_im, c_re, c_im, d_skip, w_glu):
    bsz, seq, _ = u.shape
    f32 = jnp.float32
    uf = u.astype(f32).reshape(bsz, seq, SSM_GROUPS, SSM_GROUP)
    lam_re = lam_re.astype(f32)
    lam_im = lam_im.astype(f32)
    dt = jnp.exp(log_dt.astype(f32))[:, None]
    mag = jnp.exp(lam_re * dt)
    ab_re = mag * jnp.cos(lam_im * dt)
    ab_im = mag * jnp.sin(lam_im * dt)
    nr, ni = ab_re - 1.0, ab_im
    den = lam_re * lam_re + lam_im * lam_im
    z_re = ((nr * lam_re + ni * lam_im) / den)[..., None]
    z_im = ((ni * lam_re - nr * lam_im) / den)[..., None]
    b_re = b_re.astype(f32)
    b_im = b_im.astype(f32)
    bb_re = z_re * b_re - z_im * b_im
    bb_im = z_re * b_im + z_im * b_re
    bu_re = jnp.einsum('blgh,gph->blgp', uf, bb_re)
    bu_im = jnp.einsum('blgh,gph->blgp', uf, bb_im)
    a_re = jnp.broadcast_to(ab_re, (1, seq, SSM_GROUPS, SSM_STATE))
    a_im = jnp.broadcast_to(ab_im, (1, seq, SSM_GROUPS, SSM_STATE))
    _, _, s_re, s_im = lax.associative_scan(
        _complex_affine_combine, (a_re, a_im, bu_re, bu_im), axis=1)
    y = (jnp.einsum('blgp,ghp->blgh', s_re, c_re.astype(f32))
         - jnp.einsum('blgp,ghp->blgh', s_im, c_im.astype(f32)))
    y = y.reshape(bsz, seq, BRANCH_W) + d_skip.astype(f32) * uf.reshape(bsz, seq, BRANCH_W)
    y = jax.nn.gelu(y).astype(u.dtype)
    z = y @ w_glu
    return z[..., :D_MODEL] * jax.nn.sigmoid(z[..., D_MODEL:])


def _dilated_group(q, k, v, window, dilation):
    bsz, seq, nh, hd = q.shape
    ls = seq // dilation
    nb = -(-ls // ATT_BLOCK)
    lp = nb * ATT_BLOCK
    w_sub = window // dilation

    def to_sub(t):
        t = t.reshape(bsz, ls, dilation, nh, hd).transpose(0, 2, 3, 1, 4)
        t = jnp.pad(t, ((0, 0), (0, 0), (0, 0), (0, lp - ls), (0, 0)))
        return t.reshape(bsz, dilation, nh, nb, ATT_BLOCK, hd)

    def with_prev(t):
        prev = jnp.pad(t, ((0, 0), (0, 0), (0, 0), (1, 0), (0, 0), (0, 0)))[:, :, :, :-1]
        return jnp.concatenate([prev, t], axis=4)

    qb = to_sub(q)
    kc = with_prev(to_sub(k))
    vc = with_prev(to_sub(v))
    s = jnp.einsum('bdhnqe,bdhnke->bdhnqk', qb, kc).astype(jnp.float32) * ATT_SCALE
    qi = jnp.arange(ATT_BLOCK)[:, None]
    kj = jnp.arange(2 * ATT_BLOCK)[None, :]
    dist = qi - kj + ATT_BLOCK
    kpos = jnp.arange(nb)[:, None, None] * ATT_BLOCK + kj[None] - ATT_BLOCK
    valid = (dist >= 0) & (dist <= w_sub) & (kpos >= 0)
    s = jnp.where(valid, s, -jnp.inf)
    m = jnp.max(s, axis=-1, keepdims=True)
    p = jnp.exp(s - m)
    den = jnp.sum(p, axis=-1, keepdims=True)
    o = jnp.einsum('bdhnqk,bdhnke->bdhnqe', p, vc.astype(jnp.float32)) / den
    lse = (m + jnp.log(den))[..., 0]
    o = o.reshape(bsz, dilation, nh, lp, hd)[:, :, :, :ls]
    o = o.transpose(0, 3, 1, 2, 4).reshape(bsz, seq, nh, hd)
    lse = lse.reshape(bsz, dilation, nh, lp)[:, :, :, :ls]
    lse = lse.transpose(0, 3, 1, 2).reshape(bsz, seq, nh)
    return o, lse


def dilated_attention(q, k, v):
    outs, lses = [], []
    for p_idx, (window, dilation) in enumerate(DILATED_PATTERNS):
        o, lse = _dilated_group(q[:, :, p_idx], k, v, window, dilation)
        outs.append(o)
        lses.append(lse)
    wts = jax.nn.softmax(jnp.stack(lses, axis=0), axis=0)
    return jnp.sum(wts[..., None] * jnp.stack(outs, axis=0), axis=0)


def conformer_conv(cv, conv_w, conv_b, ln_g, ln_b, w_pw2):
    a, g = jnp.split(cv, 2, axis=-1)
    h = a * jax.nn.sigmoid(g)
    h = lax.conv_general_dilated(
        h, conv_w[:, None, :], window_strides=(1,),
        padding=[(CONV_WIDTH - 1, 0)],
        dimension_numbers=('NWC', 'WIO', 'NWC'),
        feature_group_count=BRANCH_W) + conv_b
    hf = h.astype(jnp.float32)
    mu = jnp.mean(hf, axis=-1, keepdims=True)
    var = jnp.mean(jnp.square(hf - mu), axis=-1, keepdims=True)
    hn = (hf - mu) * lax.rsqrt(var + EPS) * ln_g.astype(jnp.float32) + ln_b.astype(jnp.float32)
    h = jax.nn.silu(hn).astype(cv.dtype)
    return h @ w_pw2


def setup_inputs(seed: int = 0) -> dict:
    key = jax.random.key(seed)
    ks = jax.random.split(key, 24)
    f32 = jnp.float32

    def nrm(k, shape, scale):
        return jax.random.normal(k, shape, f32) * scale

    lam_im_base = jnp.pi * jnp.arange(SSM_STATE, dtype=f32)
    return {
        'x': nrm(ks[0], (BATCH, SEQ, D_MODEL), 1.0),
        'norm1_g': 1.0 + nrm(ks[1], (DEPTH, D_MODEL), 0.05),
        'w_in': nrm(ks[2], (DEPTH, D_MODEL, IN_COLS), D_MODEL ** -0.5),
        'b_gate': nrm(ks[3], (DEPTH, COL_GATE), 0.02),
        'ssm_lambda_re': -0.5 + nrm(ks[4], (DEPTH, SSM_GROUPS, SSM_STATE), 0.01),
        'ssm_lambda_im': lam_im_base + nrm(ks[5], (DEPTH, SSM_GROUPS, SSM_STATE), 0.01),
        'ssm_log_dt': jax.random.uniform(ks[6], (DEPTH, SSM_GROUPS), f32,
                                         math.log(DT_MIN), math.log(DT_MAX)),
        'ssm_b_re': nrm(ks[7], (DEPTH, SSM_GROUPS, SSM_STATE, SSM_GROUP), (2 * SSM_GROUP) ** -0.5),
        'ssm_b_im': nrm(ks[8], (DEPTH, SSM_GROUPS, SSM_STATE, SSM_GROUP), (2 * SSM_GROUP) ** -0.5),
        'ssm_c_re': nrm(ks[9], (DEPTH, SSM_GROUPS, SSM_GROUP, SSM_STATE), SSM_STATE ** -0.25),
        'ssm_c_im': nrm(ks[10], (DEPTH, SSM_GROUPS, SSM_GROUP, SSM_STATE), SSM_STATE ** -0.25),
        'ssm_d': nrm(ks[11], (DEPTH, BRANCH_W), 1.0),
        'w_ssm_glu': nrm(ks[12], (DEPTH, BRANCH_W, 2 * D_MODEL), BRANCH_W ** -0.5),
        'w_att_up': nrm(ks[13], (DEPTH, BRANCH_W, D_MODEL), BRANCH_W ** -0.5),
        'conv_w': nrm(ks[14], (DEPTH, CONV_WIDTH, BRANCH_W), CONV_WIDTH ** -0.5),
        'conv_b': nrm(ks[15], (DEPTH, BRANCH_W), 0.02),
        'conv_ln_g': 1.0 + nrm(ks[16], (DEPTH, BRANCH_W), 0.05),
        'conv_ln_b': nrm(ks[17], (DEPTH, BRANCH_W), 0.02),
        'w_conv_pw2': nrm(ks[18], (DEPTH, BRANCH_W, D_MODEL), BRANCH_W ** -0.5),
        'w_out': nrm(ks[19], (DEPTH, D_MODEL, D_MODEL), D_MODEL ** -0.5),
        'norm2_g': 1.0 + nrm(ks[20], (DEPTH, D_MODEL), 0.05),
        'w_ffn_in': nrm(ks[21], (DEPTH, D_MODEL, 2 * D_FF), D_MODEL ** -0.5),
        'w_ffn_out': nrm(ks[22], (DEPTH, D_FF, D_MODEL), D_FF ** -0.5),
        'final_g': 1.0 + nrm(ks[23], (D_MODEL,), 0.05),
    }


def reference(x, norm1_g, w_in, b_gate, ssm_lambda_re, ssm_lambda_im, ssm_log_dt,
              ssm_b_re, ssm_b_im, ssm_c_re, ssm_c_im, ssm_d, w_ssm_glu, w_att_up,
              conv_w, conv_b, conv_ln_g, conv_ln_b, w_conv_pw2, w_out,
              norm2_g, w_ffn_in, w_ffn_out, final_g):
    bsz, seq, _ = x.shape
    for l in range(DEPTH):
        h = rms_norm(x, norm1_g[l])
        proj = h @ w_in[l]
        u, q, k, v, cv, g = jnp.split(proj, SPLIT_POINTS, axis=-1)
        y_s = s5_branch(u, ssm_lambda_re[l], ssm_lambda_im[l], ssm_log_dt[l],
                        ssm_b_re[l], ssm_b_im[l], ssm_c_re[l], ssm_c_im[l],
                        ssm_d[l], w_ssm_glu[l])
        q = q.reshape(bsz, seq, N_PAT, ATT_HEADS, HEAD_DIM)
        k = k.reshape(bsz, seq, ATT_HEADS, HEAD_DIM)
        v = v.reshape(bsz, seq, ATT_HEADS, HEAD_DIM)
        o = dilated_attention(q, k, v).astype(x.dtype).reshape(bsz, seq, BRANCH_W)
        y_a = o @ w_att_up[l]
        y_c = conformer_conv(cv, conv_w[l], conv_b[l], conv_ln_g[l], conv_ln_b[l], w_conv_pw2[l])
        gate = jax.nn.sigmoid((g + b_gate[l]).astype(jnp.float32)).astype(x.dtype)
        gate = gate.reshape(bsz, seq, N_BRANCH, D_MODEL)
        merged = gate[:, :, 0] * y_s + gate[:, :, 1] * y_a + gate[:, :, 2] * y_c
        x = x + merged @ w_out[l]
        h = rms_norm(x, norm2_g[l])
        z = h @ w_ffn_in[l]
        x = x + (jax.nn.silu(z[..., :D_FF]) * z[..., D_FF:]) @ w_ffn_out[l]
    return rms_norm(x, final_g)
```

```python
import functools
import math

import jax
import jax.numpy as jnp
from jax import lax
from jax.experimental import pallas as pl
from jax.experimental.pallas import tpu as pltpu

F32 = jnp.float32
BF16 = jnp.bfloat16

D_MODEL = 1024
DEPTH = 2
EPS = 1e-6
BRANCH_W = 512
SSM_GROUP = 16
SSM_GROUPS = 32
SSM_STATE = 64
HEAD_DIM = 64
ATT_HEADS = 8
DILATIONS = (1, 4, 16)
ATT_BLOCK = 128
ATT_SCALE = HEAD_DIM ** -0.5
CONV_WIDTH = 31
D_FF = 2816
FF_SPLITS = (0, 1536, 2816)
SSM_CHUNK = 16
CHUNK_W = SSM_CHUNK * SSM_GROUP
STATE_W = 2 * SSM_STATE
LANES = 128
CONV_HALO = 32

VMEM_LIMIT = 56 * 1024 * 1024

TM_PROJ = 512
TM_MERGE = 512
TM_FFN = 512
SSM_GROUPS_PER_STEP = 2


def _cparams(*sem):
    return pltpu.CompilerParams(dimension_semantics=sem, vmem_limit_bytes=VMEM_LIMIT)


def _const_spec(shape):
    nd = len(shape)
    return pl.BlockSpec(shape, lambda *_: (0,) * nd)


def _rms_norm(x, g):
    return x * lax.rsqrt(jnp.mean(x * x, axis=-1, keepdims=True) + EPS) * g


def _sigmoid(x):
    return 0.5 * jnp.tanh(0.5 * x) + 0.5


def _gelu_tanh(x):
    c = math.sqrt(2.0 / math.pi)
    return 0.5 * x * (1.0 + jnp.tanh(c * (x + 0.044715 * (x * x * x))))


PROJ_WIDTHS = (BRANCH_W, BRANCH_W, BRANCH_W, BRANCH_W, BRANCH_W, BRANCH_W, 2 * BRANCH_W, 3 * D_MODEL)


def _inproj_kernel(x_ref, g_ref, w_ref, *out_refs):
    h = _rms_norm(x_ref[...], g_ref[...]).astype(BF16)
    col = 0
    for idx, ref in enumerate(out_refs):
        width = ref.shape[-1]
        scale = ATT_SCALE if idx in (1, 2, 3) else None
        for c0 in range(0, width, BRANCH_W):
            acc = jnp.dot(h, w_ref[:, col + c0:col + c0 + BRANCH_W], preferred_element_type=F32)
            if scale is not None:
                acc = acc * scale
            ref[:, c0:c0 + BRANCH_W] = acc.astype(BF16)
        col += width


def in_projection(x2, g, w_bf16):
    n = x2.shape[0]
    tm = TM_PROJ
    in_cols = w_bf16.shape[1]
    return pl.pallas_call(
        _inproj_kernel,
        grid=(n // tm,),
        in_specs=[pl.BlockSpec((tm, D_MODEL), lambda i: (i, 0)),
                  _const_spec((1, D_MODEL)),
                  _const_spec((D_MODEL, in_cols))],
        out_specs=[pl.BlockSpec((tm, w), lambda i: (i, 0)) for w in PROJ_WIDTHS],
        out_shape=[jax.ShapeDtypeStruct((n, w), BF16) for w in PROJ_WIDTHS],
        compiler_params=_cparams("parallel"),
        name="in_projection",
    )(x2, g.reshape(1, D_MODEL), w_bf16)


def ssm_matrices(lam_re, lam_im, log_dt, b_re, b_im, c_re, c_im, d_skip):
    hi = lax.Precision.HIGHEST
    t = SSM_CHUNK
    dt = jnp.exp(log_dt.astype(F32))[:, None]
    lam_re = lam_re.astype(F32)
    lam_im = lam_im.astype(F32)
    mag = jnp.exp(lam_re * dt)
    ab_re = mag * jnp.cos(lam_im * dt)
    ab_im = mag * jnp.sin(lam_im * dt)
    nr, ni = ab_re - 1.0, ab_im
    den = lam_re * lam_re + lam_im * lam_im
    z_re = ((nr * lam_re + ni * lam_im) / den)[..., None]
    z_im = ((ni * lam_re - nr * lam_im) / den)[..., None]
    b_re = b_re.astype(F32)
    b_im = b_im.astype(F32)
    bb_re = z_re * b_re - z_im * b_im
    bb_im = z_re * b_im + z_im * b_re
    n = jnp.arange(t + 1, dtype=F32)[:, None, None]
    pmag = jnp.exp(n * (lam_re * dt)[None])
    pw_re = pmag * jnp.cos(n * (lam_im * dt)[None])
    pw_im = pmag * jnp.sin(n * (lam_im * dt)[None])
    c_re = c_re.astype(F32)
    c_im = c_im.astype(F32)
    ca_re = c_re[None] * pw_re[:, :, None, :] - c_im[None] * pw_im[:, :, None, :]
    ca_im = c_re[None] * pw_im[:, :, None, :] + c_im[None] * pw_re[:, :, None, :]
    kmat = (jnp.einsum('dgcp,gph->dgch', ca_re[:t], bb_re, precision=hi)
            - jnp.einsum('dgcp,gph->dgch', ca_im[:t], bb_im, precision=hi))
    eye = jnp.eye(SSM_GROUP, dtype=F32)
    kmat = kmat.at[0].add(d_skip.astype(F32).reshape(SSM_GROUPS, SSM_GROUP)[:, :, None] * eye[None])
    kext = jnp.concatenate([kmat, jnp.zeros_like(kmat[:1])], axis=0)
    jj = jnp.arange(t)
    lag = jj[None, :] - jj[:, None]
    lag = jnp.where(lag >= 0, lag, t)
    m5 = kext[lag]
    m_toep = m5.transpose(2, 0, 4, 1, 3).reshape(SSM_GROUPS, CHUNK_W, CHUNK_W)
    rev_re = pw_re[:t][::-1]
    rev_im = pw_im[:t][::-1]
    wb_re = rev_re[..., None] * bb_re[None] - rev_im[..., None] * bb_im[None]
    wb_im = rev_re[..., None] * bb_im[None] + rev_im[..., None] * bb_re[None]
    wb = jnp.concatenate([wb_re, wb_im], axis=2)
    w_b = wb.transpose(1, 0, 3, 2).reshape(SSM_GROUPS, CHUNK_W, STATE_W)
    wc = jnp.concatenate([ca_re[1:], -ca_im[1:]], axis=3)
    w_c = wc.transpose(1, 3, 0, 2).reshape(SSM_GROUPS, STATE_W, CHUNK_W)
    a1 = jnp.concatenate([pw_re[t], pw_re[t]], axis=-1)[:, None, :]
    a2 = jnp.concatenate([-pw_im[t], pw_im[t]], axis=-1)[:, None, :]
    return m_toep.astype(BF16), w_b.astype(BF16), w_c.astype(BF16), a1, a2


def _ssm_kernel(u_ref, m_ref, wb_ref, wc_ref, a1_ref, a2_ref, y_ref, x_scr, sp_scr, *, n_chunks, n_batch):
    gb = u_ref.shape[0]
    for g in range(gb):
        x_scr[g] = jnp.dot(u_ref[g], wb_ref[g], preferred_element_type=F32)
    a1 = jnp.broadcast_to(a1_ref[...], (gb, n_batch, STATE_W))
    a2 = jnp.broadcast_to(a2_ref[...], (gb, n_batch, STATE_W))

    def step(s, state):
        r0 = pl.multiple_of(s * n_batch, n_batch)
        sp_scr[:, pl.ds(r0, n_batch), :] = state
        swapped = pltpu.roll(state, SSM_STATE, axis=2)
        return a1 * state + a2 * swapped + x_scr[:, pl.ds(r0, n_batch), :]

    lax.fori_loop(0, n_chunks, step, jnp.zeros((gb, n_batch, STATE_W), F32))
    for g in range(gb):
        y = (jnp.dot(u_ref[g], m_ref[g], preferred_element_type=F32)
             + jnp.dot(sp_scr[g].astype(BF16), wc_ref[g], preferred_element_type=F32))
        y_ref[g] = _gelu_tanh(y).astype(BF16)


def ssm_mixer(u, mats, bsz, seq):
    m_toep, w_b, w_c, a1, a2 = mats
    ns = seq // SSM_CHUNK
    rows = ns * bsz
    gb = SSM_GROUPS_PER_STEP
    ug = u.reshape(bsz, ns, SSM_CHUNK, SSM_GROUPS, SSM_GROUP).transpose(3, 1, 0, 2, 4)
    ug = ug.reshape(SSM_GROUPS, rows, CHUNK_W)
    gspec = lambda shape: pl.BlockSpec((gb,) + shape, lambda i: (i, 0, 0))
    yg = pl.pallas_call(
        functools.partial(_ssm_kernel, n_chunks=ns, n_batch=bsz),
        grid=(SSM_GROUPS // gb,),
        in_specs=[gspec((rows, CHUNK_W)), gspec((CHUNK_W, CHUNK_W)), gspec((CHUNK_W, STATE_W)),
                  gspec((STATE_W, CHUNK_W)), gspec((1, STATE_W)), gspec((1, STATE_W))],
        out_specs=gspec((rows, CHUNK_W)),
        out_shape=jax.ShapeDtypeStruct((SSM_GROUPS, rows, CHUNK_W), BF16),
        scratch_shapes=[pltpu.VMEM((gb, rows, STATE_W), F32), pltpu.VMEM((gb, rows, STATE_W), F32)],
        compiler_params=_cparams("parallel"),
        name="ssm_mixer",
    )(ug, m_toep, w_b, w_c, a1, a2)
    y = yg.reshape(SSM_GROUPS, ns, bsz, SSM_CHUNK, SSM_GROUP).transpose(2, 1, 3, 0, 4)
    return y.reshape(bsz * seq, BRANCH_W)


def _attn_block(q2, kc, vc, valid, lane_lo):
    zero = jnp.zeros_like(q2)
    qq = jnp.concatenate([jnp.where(lane_lo, q2, zero), jnp.where(lane_lo, zero, q2)], axis=0)
    s = lax.dot_general(qq, kc, (((1,), (1,)), ((), ())), preferred_element_type=F32)
    ps, dens, lses = [], [], []
    for half in range(2):
        sh = jnp.where(valid, s[half * ATT_BLOCK:(half + 1) * ATT_BLOCK], -jnp.inf)
        m = jnp.max(sh, axis=1, keepdims=True)
        p = jnp.exp(sh - m)
        den = jnp.sum(p, axis=1, keepdims=True)
        ps.append(p.astype(BF16))
        dens.append(den)
        lses.append(m + jnp.log(den))
    pv = jnp.dot(jnp.concatenate(ps, axis=0), vc, preferred_element_type=F32)
    o = jnp.where(lane_lo, pv[:ATT_BLOCK] / dens[0], pv[ATT_BLOCK:] / dens[1])
    return o, lses[0], lses[1]


def _attn_kernel(q_ref, k_ref, v_ref, o_ref, lse_ref, *, dilation, n_blocks):
    blk = ATT_BLOCK
    lane = lax.broadcasted_iota(jnp.int32, (blk, LANES), 1)
    lane_lo = lane < HEAD_DIM
    qi = lax.broadcasted_iota(jnp.int32, (blk, 2 * blk), 0)
    kj = lax.broadcasted_iota(jnp.int32, (blk, 2 * blk), 1)
    dist = qi - kj + blk
    valid2 = (dist >= 0) & (dist <= blk)
    valid1 = (lax.broadcasted_iota(jnp.int32, (blk, blk), 1)
              <= lax.broadcasted_iota(jnp.int32, (blk, blk), 0))
    head_lane = lax.broadcasted_iota(jnp.int32, (blk, ATT_HEADS), 1)

    def do_block(row0, col_base, first):
        lse_rows = jnp.zeros((blk, ATT_HEADS), F32)
        for hp in range(ATT_HEADS // 2):
            c0 = col_base + hp * LANES
            q2 = q_ref[pl.ds(row0, blk), c0:c0 + LANES]
            if first:
                kc = k_ref[pl.ds(row0, blk), c0:c0 + LANES]
                vc = v_ref[pl.ds(row0, blk), c0:c0 + LANES]
                o, la, lb = _attn_block(q2, kc, vc, valid1, lane_lo)
            else:
                kc = k_ref[pl.ds(row0 - blk, 2 * blk), c0:c0 + LANES]
                vc = v_ref[pl.ds(row0 - blk, 2 * blk), c0:c0 + LANES]
                o, la, lb = _attn_block(q2, kc, vc, valid2, lane_lo)
            o_ref[pl.ds(row0, blk), c0:c0 + LANES] = o.astype(o_ref.dtype)
            lse_rows = jnp.where(head_lane == 2 * hp, la, lse_rows)
            lse_rows = jnp.where(head_lane == 2 * hp + 1, lb, lse_rows)
        return lse_rows

    for r in range(dilation):
        col_base = r * BRANCH_W
        lcol = r * ATT_HEADS
        lse_ref[0:blk, lcol:lcol + ATT_HEADS] = do_block(0, col_base, True)
        if n_blocks > 1:
            def body(n, carry):
                row0 = pl.multiple_of(n * blk, blk)
                lse_ref[pl.ds(row0, blk), lcol:lcol + ATT_HEADS] = do_block(row0, col_base, False)
                return carry
            lax.fori_loop(1, n_blocks, body, 0)


def dilated_attention_pattern(q, k, v, bsz, seq, dilation):
    ls = seq // dilation
    width = dilation * BRANCH_W
    view = lambda t: t.reshape(bsz, ls, width)
    spec = pl.BlockSpec((None, ls, width), lambda b: (b, 0, 0))
    lse_w = dilation * ATT_HEADS
    o, lse = pl.pallas_call(
        functools.partial(_attn_kernel, dilation=dilation, n_blocks=ls // ATT_BLOCK),
        grid=(bsz,),
        in_specs=[spec, spec, spec],
        out_specs=[spec, pl.BlockSpec((None, ls, lse_w), lambda b: (b, 0, 0))],
        out_shape=[jax.ShapeDtypeStruct((bsz, ls, width), BF16),
                   jax.ShapeDtypeStruct((bsz, ls, lse_w), F32)],
        compiler_params=_cparams("parallel"),
        name=f"dilated_attention_d{dilation}",
    )(view(q), view(k), view(v))
    return o.reshape(bsz * seq, BRANCH_W), lse.reshape(bsz * seq, ATT_HEADS)


def _merge_kernel(x_ref, ys_ref, o0_ref, o1_ref, o2_ref, l0_ref, l1_ref, l2_ref, cv_ref, halo_ref, gate_ref,
                  bgate_ref, wglu_ref, watt_ref, convw_ref, convb_ref, lng_ref, lnb_ref, wpw2_ref, wout_ref,
                  out_ref, conv_scr, *, tiles_per_seq):
    tm = x_ref.shape[0]
    z = jnp.dot(ys_ref[...], wglu_ref[...], preferred_element_type=F32)
    y_s = z[:, :D_MODEL] * _sigmoid(z[:, D_MODEL:])

    lses = [l0_ref[...], l1_ref[...], l2_ref[...]]
    m = jnp.maximum(jnp.maximum(lses[0], lses[1]), lses[2])
    es = [jnp.exp(l - m) for l in lses]
    inv = 1.0 / (es[0] + es[1] + es[2])
    lane_lo = lax.broadcasted_iota(jnp.int32, (tm, LANES), 1) < HEAD_DIM
    o_refs = (o0_ref, o1_ref, o2_ref)
    pieces = []
    for hp in range(ATT_HEADS // 2):
        acc = jnp.zeros((tm, LANES), F32)
        for p in range(3):
            w = es[p] * inv
            wexp = jnp.where(lane_lo, w[:, 2 * hp:2 * hp + 1], w[:, 2 * hp + 1:2 * hp + 2])
            acc = acc + wexp * o_refs[p][:, hp * LANES:(hp + 1) * LANES].astype(F32)
        pieces.append(acc.astype(BF16))
    o = jnp.concatenate(pieces, axis=1)
    y_a = jnp.dot(o, watt_ref[...], preferred_element_type=F32)

    def glu(t):
        t = t.astype(F32)
        return t[:, :BRANCH_W] * _sigmoid(t[:, BRANCH_W:])
    first = (pl.program_id(0) % tiles_per_seq) == 0
    halo = glu(halo_ref[...])
    conv_scr[0:CONV_HALO, :] = jnp.where(first, jnp.zeros_like(halo), halo)
    conv_scr[CONV_HALO:CONV_HALO + tm, :] = glu(cv_ref[...])
    acc = jnp.zeros((tm, BRANCH_W), F32) + convb_ref[...]
    base = CONV_HALO - (CONV_WIDTH - 1)
    for tap in range(CONV_WIDTH):
        acc = acc + convw_ref[tap:tap + 1, :] * conv_scr[base + tap:base + tap + tm, :]
    mu = jnp.mean(acc, axis=-1, keepdims=True)
    cen = acc - mu
    var = jnp.mean(cen * cen, axis=-1, keepdims=True)
    hn = cen * lax.rsqrt(var + EPS) * lng_ref[...] + lnb_ref[...]
    hc = (hn * _sigmoid(hn)).astype(BF16)
    y_c = jnp.dot(hc, wpw2_ref[...], preferred_element_type=F32)

    def gate(i):
        return _sigmoid(gate_ref[:, i * D_MODEL:(i + 1) * D_MODEL].astype(F32)
                        + bgate_ref[:, i * D_MODEL:(i + 1) * D_MODEL])
    merged = gate(0) * y_s + gate(1) * y_a + gate(2) * y_c
    out_ref[...] = x_ref[...] + jnp.dot(merged.astype(BF16), wout_ref[...], preferred_element_type=F32)


def merge_branches(x2, ys, os_, lses, cv, gates, b_gate, w_glu, w_att, conv_w, conv_b, ln_g, ln_b,
                   w_pw2, w_out, seq):
    n = x2.shape[0]
    tm = TM_MERGE
    row = lambda w: pl.BlockSpec((tm, w), lambda i: (i, 0))
    halo_blocks = tm // CONV_HALO
    halo_spec = pl.BlockSpec((CONV_HALO, 2 * BRANCH_W), lambda i: (jnp.maximum(i * halo_blocks - 1, 0), 0))
    return pl.pallas_call(
        functools.partial(_merge_kernel, tiles_per_seq=seq // tm),
        grid=(n // tm,),
        in_specs=[row(D_MODEL), row(BRANCH_W), row(BRANCH_W), row(BRANCH_W), row(BRANCH_W),
                  row(ATT_HEADS), row(ATT_HEADS), row(ATT_HEADS),
                  row(2 * BRANCH_W), halo_spec, row(3 * D_MODEL),
                  _const_spec((1, 3 * D_MODEL)), _const_spec((BRANCH_W, 2 * D_MODEL)),
                  _const_spec((BRANCH_W, D_MODEL)), _const_spec((CONV_WIDTH, BRANCH_W)),
                  _const_spec((1, BRANCH_W)), _const_spec((1, BRANCH_W)), _const_spec((1, BRANCH_W)),
                  _const_spec((BRANCH_W, D_MODEL)), _const_spec((D_MODEL, D_MODEL))],
        out_specs=row(D_MODEL),
        out_shape=jax.ShapeDtypeStruct((n, D_MODEL), F32),
        scratch_shapes=[pltpu.VMEM((CONV_HALO + tm, BRANCH_W), F32)],
        compiler_params=_cparams("parallel"),
        name="merge_branches",
    )(x2, ys, os_[0], os_[1], os_[2], lses[0], lses[1], lses[2], cv, cv, gates,
      b_gate.reshape(1, -1).astype(F32), w_glu, w_att, conv_w.astype(F32),
      conv_b.reshape(1, -1).astype(F32), ln_g.reshape(1, -1).astype(F32), ln_b.reshape(1, -1).astype(F32),
      w_pw2, w_out)


def _ffn_kernel(x_ref, g_ref, w1_ref, w2_ref, fg_ref, out_ref, *, final_norm):
    x = x_ref[...]
    h = _rms_norm(x, g_ref[...]).astype(BF16)
    acc = x
    for c0, c1 in zip(FF_SPLITS[:-1], FF_SPLITS[1:]):
        za = jnp.dot(h, w1_ref[:, c0:c1], preferred_element_type=F32)
        zb = jnp.dot(h, w1_ref[:, D_FF + c0:D_FF + c1], preferred_element_type=F32)
        act = (za * _sigmoid(za) * zb).astype(BF16)
        acc = acc + jnp.dot(act, w2_ref[c0:c1, :], preferred_element_type=F32)
    if final_norm:
        acc = _rms_norm(acc, fg_ref[...])
    out_ref[...] = acc


def ffn(x2, g, w1, w2, final_g, final_norm):
    n = x2.shape[0]
    tm = TM_FFN
    return pl.pallas_call(
        functools.partial(_ffn_kernel, final_norm=final_norm),
        grid=(n // tm,),
        in_specs=[pl.BlockSpec((tm, D_MODEL), lambda i: (i, 0)), _const_spec((1, D_MODEL)),
                  _const_spec((D_MODEL, 2 * D_FF)), _const_spec((D_FF, D_MODEL)), _const_spec((1, D_MODEL))],
        out_specs=pl.BlockSpec((tm, D_MODEL), lambda i: (i, 0)),
        out_shape=jax.ShapeDtypeStruct((n, D_MODEL), F32),
        compiler_params=_cparams("parallel"),
        name="ffn",
    )(x2, g.reshape(1, D_MODEL), w1, w2, final_g.reshape(1, D_MODEL))


def kernel(x, norm1_g, w_in, b_gate, ssm_lambda_re, ssm_lambda_im, ssm_log_dt, ssm_b_re, ssm_b_im, ssm_c_re,
           ssm_c_im, ssm_d, w_ssm_glu, w_att_up, conv_w, conv_b, conv_ln_g, conv_ln_b, w_conv_pw2, w_out,
           norm2_g, w_ffn_in, w_ffn_out, final_g):
    bsz, seq, _ = x.shape
    assert seq % (max(DILATIONS) * ATT_BLOCK) == 0 and seq % TM_MERGE == 0 and bsz % 8 == 0
    x2 = x.reshape(bsz * seq, D_MODEL).astype(F32)
    for l in range(DEPTH):
        u, q0, q1, q2, k, v, cv, gates = in_projection(x2, norm1_g[l], w_in[l].astype(BF16))
        mats = ssm_matrices(ssm_lambda_re[l], ssm_lambda_im[l], ssm_log_dt[l], ssm_b_re[l], ssm_b_im[l],
                            ssm_c_re[l], ssm_c_im[l], ssm_d[l])
        ys = ssm_mixer(u, mats, bsz, seq)
        os_, lses = [], []
        for q, dil in zip((q0, q1, q2), DILATIONS):
            o, lse = dilated_attention_pattern(q, k, v, bsz, seq, dil)
            os_.append(o)
            lses.append(lse)
        x2 = merge_branches(x2, ys, os_, lses, cv, gates, b_gate[l], w_ssm_glu[l].astype(BF16),
                            w_att_up[l].astype(BF16), conv_w[l], conv_b[l], conv_ln_g[l], conv_ln_b[l],
                            w_conv_pw2[l].astype(BF16), w_out[l].astype(BF16), seq)
        x2 = ffn(x2, norm2_g[l], w_ffn_in[l].astype(BF16), w_ffn_out[l].astype(BF16), final_g,
                 final_norm=(l == DEPTH - 1))
    return x2.reshape(bsz, seq, D_MODEL)
```

```python
import functools
import math

import jax
import jax.numpy as jnp
from jax import lax
from jax.experimental import pallas as pl
from jax.experimental.pallas import tpu as pltpu

F32 = jnp.float32
BF16 = jnp.bfloat16

D_MODEL = 1024
DEPTH = 2
EPS = 1e-6
BRANCH_W = 512
SSM_GROUP = 16
SSM_GROUPS = 32
SSM_STATE = 64
HEAD_DIM = 64
ATT_HEADS = 8
DILATIONS = (1, 4, 16)
ATT_BLOCK = 128
ATT_SCALE = HEAD_DIM ** -0.5
CONV_WIDTH = 31
D_FF = 2816
FF_SPLITS = (0, 1536, 2816)
SSM_CHUNK = 16
CHUNK_W = SSM_CHUNK * SSM_GROUP
STATE_W = 2 * SSM_STATE
LANES = 128
CONV_HALO = 32

VMEM_LIMIT = 56 * 1024 * 1024

TM_PROJ = 512
TM_MERGE = 512
TM_FFN = 512


def _cparams(*sem):
    return pltpu.CompilerParams(dimension_semantics=sem, vmem_limit_bytes=VMEM_LIMIT)


def _const_spec(shape):
    nd = len(shape)
    return pl.BlockSpec(shape, lambda *_: (0,) * nd)


def _rms_norm(x, g):
    return x * lax.rsqrt(jnp.mean(x * x, axis=-1, keepdims=True) + EPS) * g


def _sigmoid(x):
    return 0.5 * jnp.tanh(0.5 * x) + 0.5


def _gelu_tanh(x):
    c = math.sqrt(2.0 / math.pi)
    return 0.5 * x * (1.0 + jnp.tanh(c * (x + 0.044715 * (x * x * x))))


N_SLABS = BRANCH_W // LANES
STRIPS = LANES // SSM_GROUP


def _strip_id(rows):
    return lax.broadcasted_iota(jnp.int32, (rows, LANES), 1) // SSM_GROUP


def _gather_strips(srcs, dst_strip_of_src, src_strip, strip_id):
    acc = None
    for src, ds, ss in zip(srcs, dst_strip_of_src, src_strip):
        shift = ((ds - ss) * SSM_GROUP) % LANES
        moved = src if shift == 0 else pltpu.roll(src, shift, axis=1)
        acc = moved if acc is None else jnp.where(strip_id == ds, moved, acc)
    return acc


def _inproj_kernel(x_ref, g_ref, w_ref, u_ref, q0_ref, q1_ref, q2_ref, k_ref, v_ref, cv_ref, gate_ref, u_scr):
    tm = x_ref.shape[0]
    h = _rms_norm(x_ref[...], g_ref[...]).astype(BF16)

    def proj(col):
        return jnp.dot(h, w_ref[:, col:col + BRANCH_W], preferred_element_type=F32)

    acc = proj(0)
    for sl in range(N_SLABS):
        u_scr[sl] = acc[:, sl * LANES:(sl + 1) * LANES]
    n_rows = tm // SSM_CHUNK
    strip_id = _strip_id(n_rows)
    for sl in range(N_SLABS):
        toks = [u_scr[sl, pl.ds(j, n_rows, stride=SSM_CHUNK), :] for j in range(SSM_CHUNK)]
        for gl in range(STRIPS):
            for jo in range(SSM_CHUNK // STRIPS):
                srcs = toks[jo * STRIPS:(jo + 1) * STRIPS]
                tile = _gather_strips(srcs, list(range(STRIPS)), [gl] * STRIPS, strip_id)
                lane0 = (sl * STRIPS + gl) * CHUNK_W + jo * LANES
                u_ref[:, lane0:lane0 + LANES] = tile.astype(BF16)
    col = BRANCH_W
    for ref, scale in ((q0_ref, ATT_SCALE), (q1_ref, ATT_SCALE), (q2_ref, ATT_SCALE), (k_ref, None), (v_ref, None)):
        acc = proj(col)
        if scale is not None:
            acc = acc * scale
        for sl in range(N_SLABS):
            ref[sl] = acc[:, sl * LANES:(sl + 1) * LANES]
        col += BRANCH_W
    for ref in (cv_ref, gate_ref):
        for c0 in range(0, ref.shape[-1], BRANCH_W):
            ref[:, c0:c0 + BRANCH_W] = proj(col).astype(BF16)
            col += BRANCH_W


def in_projection(x2, g, w_bf16):
    n = x2.shape[0]
    tm = TM_PROJ
    in_cols = w_bf16.shape[1]
    slab_spec = pl.BlockSpec((N_SLABS, tm, LANES), lambda i: (0, i, 0))
    slab_shape = jax.ShapeDtypeStruct((N_SLABS, n, LANES), F32)
    u_w = SSM_GROUPS * CHUNK_W
    return pl.pallas_call(
        _inproj_kernel,
        grid=(n // tm,),
        in_specs=[pl.BlockSpec((tm, D_MODEL), lambda i: (i, 0)),
                  _const_spec((1, D_MODEL)),
                  _const_spec((D_MODEL, in_cols))],
        out_specs=[pl.BlockSpec((tm // SSM_CHUNK, u_w), lambda i: (i, 0))] + [slab_spec] * 5
                  + [pl.BlockSpec((tm, 2 * BRANCH_W), lambda i: (i, 0)),
                     pl.BlockSpec((tm, 3 * D_MODEL), lambda i: (i, 0))],
        out_shape=[jax.ShapeDtypeStruct((n // SSM_CHUNK, u_w), BF16)] + [slab_shape] * 5
                  + [jax.ShapeDtypeStruct((n, 2 * BRANCH_W), BF16), jax.ShapeDtypeStruct((n, 3 * D_MODEL), BF16)],
        scratch_shapes=[pltpu.VMEM((N_SLABS, tm, LANES), F32)],
        compiler_params=_cparams("parallel"),
        name="in_projection",
    )(x2, g.reshape(1, D_MODEL), w_bf16)


def ssm_matrices(lam_re, lam_im, log_dt, b_re, b_im, c_re, c_im, d_skip):
    hi = lax.Precision.HIGHEST
    t = SSM_CHUNK
    dt = jnp.exp(log_dt.astype(F32))[:, None]
    lam_re = lam_re.astype(F32)
    lam_im = lam_im.astype(F32)
    mag = jnp.exp(lam_re * dt)
    ab_re = mag * jnp.cos(lam_im * dt)
    ab_im = mag * jnp.sin(lam_im * dt)
    nr, ni = ab_re - 1.0, ab_im
    den = lam_re * lam_re + lam_im * lam_im
    z_re = ((nr * lam_re + ni * lam_im) / den)[..., None]
    z_im = ((ni * lam_re - nr * lam_im) / den)[..., None]
    b_re = b_re.astype(F32)
    b_im = b_im.astype(F32)
    bb_re = z_re * b_re - z_im * b_im
    bb_im = z_re * b_im + z_im * b_re
    n = jnp.arange(t + 1, dtype=F32)[:, None, None]
    pmag = jnp.exp(n * (lam_re * dt)[None])
    pw_re = pmag * jnp.cos(n * (lam_im * dt)[None])
    pw_im = pmag * jnp.sin(n * (lam_im * dt)[None])
    c_re = c_re.astype(F32)
    c_im = c_im.astype(F32)
    ca_re = c_re[None] * pw_re[:, :, None, :] - c_im[None] * pw_im[:, :, None, :]
    ca_im = c_re[None] * pw_im[:, :, None, :] + c_im[None] * pw_re[:, :, None, :]
    kmat = (jnp.einsum('dgcp,gph->dgch', ca_re[:t], bb_re, precision=hi)
            - jnp.einsum('dgcp,gph->dgch', ca_im[:t], bb_im, precision=hi))
    eye = jnp.eye(SSM_GROUP, dtype=F32)
    kmat = kmat.at[0].add(d_skip.astype(F32).reshape(SSM_GROUPS, SSM_GROUP)[:, :, None] * eye[None])
    kext = jnp.concatenate([kmat, jnp.zeros_like(kmat[:1])], axis=0)
    jj = jnp.arange(t)
    lag = jj[None, :] - jj[:, None]
    lag = jnp.where(lag >= 0, lag, t)
    m5 = kext[lag]
    m_toep = m5.transpose(2, 0, 4, 1, 3).reshape(SSM_GROUPS, CHUNK_W, CHUNK_W)
    rev_re = pw_re[:t][::-1]
    rev_im = pw_im[:t][::-1]
    wb_re = rev_re[..., None] * bb_re[None] - rev_im[..., None] * bb_im[None]
    wb_im = rev_re[..., None] * bb_im[None] + rev_im[..., None] * bb_re[None]
    wb = jnp.concatenate([wb_re, wb_im], axis=2)
    w_b = wb.transpose(1, 0, 3, 2).reshape(SSM_GROUPS, CHUNK_W, STATE_W)
    wc = jnp.concatenate([ca_re[1:], -ca_im[1:]], axis=3)
    w_c = wc.transpose(1, 3, 0, 2).reshape(SSM_GROUPS, STATE_W, CHUNK_W)
    a1 = jnp.concatenate([pw_re[t], pw_re[t]], axis=-1)[:, None, :]
    a2 = jnp.concatenate([-pw_im[t], pw_im[t]], axis=-1)[:, None, :]
    return m_toep.astype(BF16), w_b.astype(BF16), w_c.astype(BF16), a1, a2


SSM_BATCH = 8
SSM_PITCH = 136
SSM_OUT_ROWS = 512


def _ssm_kernel(u_ref, m_ref, wb_ref, wc_ref, a1_ref, a2_ref, y_ref, x_scr, sp_scr, y_scr, *, n_chunks):
    nb = SSM_BATCH
    rows = nb * n_chunks
    for g in range(STRIPS):
        x = jnp.dot(u_ref[:, g * CHUNK_W:(g + 1) * CHUNK_W], wb_ref[g], preferred_element_type=F32)
        for b in range(nb):
            x_scr[g, b * SSM_PITCH:b * SSM_PITCH + n_chunks, :] = x[b * n_chunks:(b + 1) * n_chunks]
    a1 = [jnp.broadcast_to(a1_ref[g], (nb, STATE_W)) for g in range(STRIPS)]
    a2 = [jnp.broadcast_to(a2_ref[g], (nb, STATE_W)) for g in range(STRIPS)]

    def step(s, states):
        new = []
        for g in range(STRIPS):
            st = states[g]
            sp_scr[g, pl.ds(s, nb, stride=n_chunks), :] = st
            swapped = pltpu.roll(st, SSM_STATE, axis=1)
            new.append(a1[g] * st + a2[g] * swapped + x_scr[g, pl.ds(s, nb, stride=SSM_PITCH), :])
        return tuple(new)

    lax.fori_loop(0, n_chunks, step, tuple(jnp.zeros((nb, STATE_W), F32) for _ in range(STRIPS)))

    rc = SSM_OUT_ROWS
    strip_id = _strip_id(rc)
    for r0 in range(0, rows, rc):
        for g in range(STRIPS):
            y = (jnp.dot(u_ref[r0:r0 + rc, g * CHUNK_W:(g + 1) * CHUNK_W], m_ref[g], preferred_element_type=F32)
                 + jnp.dot(sp_scr[g, r0:r0 + rc, :].astype(BF16), wc_ref[g], preferred_element_type=F32))
            y_scr[g] = _gelu_tanh(y)
        for j in range(SSM_CHUNK):
            jo, k = divmod(j, STRIPS)
            srcs = [y_scr[g, :, jo * LANES:(jo + 1) * LANES] for g in range(STRIPS)]
            tile = _gather_strips(srcs, list(range(STRIPS)), [k] * STRIPS, strip_id)
            y_ref[pl.ds(r0 * SSM_CHUNK + j, rc, stride=SSM_CHUNK), :] = tile


def ssm_mixer(u_chunks, mats, bsz, seq):
    m_toep, w_b, w_c, a1, a2 = mats
    ns = seq // SSM_CHUNK
    nb = SSM_BATCH
    gw = STRIPS * CHUNK_W
    gspec = lambda shape: pl.BlockSpec((STRIPS,) + shape, lambda i, j: (j, 0, 0))
    return pl.pallas_call(
        functools.partial(_ssm_kernel, n_chunks=ns),
        grid=(bsz // nb, SSM_GROUPS // STRIPS),
        in_specs=[pl.BlockSpec((nb * ns, gw), lambda i, j: (i, j)),
                  gspec((CHUNK_W, CHUNK_W)), gspec((CHUNK_W, STATE_W)),
                  gspec((STATE_W, CHUNK_W)), gspec((1, STATE_W)), gspec((1, STATE_W))],
        out_specs=pl.BlockSpec((nb * seq, LANES), lambda i, j: (i, j)),
        out_shape=jax.ShapeDtypeStruct((bsz * seq, BRANCH_W), F32),
        scratch_shapes=[pltpu.VMEM((STRIPS, nb * SSM_PITCH, STATE_W), F32),
                        pltpu.VMEM((STRIPS, nb * ns, STATE_W), F32),
                        pltpu.VMEM((STRIPS, SSM_OUT_ROWS, CHUNK_W), F32)],
        compiler_params=_cparams("parallel", "parallel"),
        name="ssm_mixer",
    )(u_chunks, m_toep, w_b, w_c, a1, a2)


def _attn_block(q2, kc, vc, valid, lane_lo):
    zero = jnp.zeros_like(q2)
    qq = jnp.concatenate([jnp.where(lane_lo, q2, zero), jnp.where(lane_lo, zero, q2)], axis=0)
    s = lax.dot_general(qq, kc, (((1,), (1,)), ((), ())), preferred_element_type=F32)
    ps, dens, lses = [], [], []
    for half in range(2):
        sh = jnp.where(valid, s[half * ATT_BLOCK:(half + 1) * ATT_BLOCK], -jnp.inf)
        m = jnp.max(sh, axis=1, keepdims=True)
        p = jnp.exp(sh - m)
        den = jnp.sum(p, axis=1, keepdims=True)
        ps.append(p.astype(BF16))
        dens.append(den)
        lses.append(m + jnp.log(den))
    pv = jnp.dot(jnp.concatenate(ps, axis=0), vc, preferred_element_type=F32)
    o = jnp.where(lane_lo, pv[:ATT_BLOCK] / dens[0], pv[ATT_BLOCK:] / dens[1])
    return o, lses[0], lses[1]


def _attn_kernel(q_ref, k_ref, v_ref, o_ref, lse_ref, *, dilation, n_blocks):
    blk = ATT_BLOCK
    lane = lax.broadcasted_iota(jnp.int32, (blk, LANES), 1)
    lane_lo = lane < HEAD_DIM
    qi = lax.broadcasted_iota(jnp.int32, (blk, 2 * blk), 0)
    kj = lax.broadcasted_iota(jnp.int32, (blk, 2 * blk), 1)
    dist = qi - kj + blk
    valid2 = (dist >= 0) & (dist <= blk)
    valid1 = (lax.broadcasted_iota(jnp.int32, (blk, blk), 1)
              <= lax.broadcasted_iota(jnp.int32, (blk, blk), 0))
    stride = None if dilation == 1 else dilation

    def rows(start, n):
        return pl.ds(start, n) if stride is None else pl.ds(start, n, stride=stride)

    def do_block(start, first):
        lse_rows = jnp.zeros((blk, LANES), F32)
        for hp in range(N_SLABS):
            q2 = q_ref[hp, rows(start, blk), :].astype(BF16)
            kc = k_ref[hp, rows(start, blk), :].astype(BF16)
            vc = v_ref[hp, rows(start, blk), :].astype(BF16)
            if first:
                o, la, lb = _attn_block(q2, kc, vc, valid1, lane_lo)
            else:
                prev = start - blk * dilation
                kc = jnp.concatenate([k_ref[hp, rows(prev, blk), :].astype(BF16), kc], axis=0)
                vc = jnp.concatenate([v_ref[hp, rows(prev, blk), :].astype(BF16), vc], axis=0)
                o, la, lb = _attn_block(q2, kc, vc, valid2, lane_lo)
            o_ref[hp, rows(start, blk), :] = o
            lse_rows = jnp.where(lane == 2 * hp, la, lse_rows)
            lse_rows = jnp.where(lane == 2 * hp + 1, lb, lse_rows)
        lse_ref[rows(start, blk), :] = lse_rows

    for r in range(dilation):
        do_block(r, True)
        if n_blocks > 1:
            def body(n, carry):
                do_block(r + n * (blk * dilation), False)
                return carry
            lax.fori_loop(1, n_blocks, body, 0)


def dilated_attention_pattern(q, k, v, bsz, seq, dilation):
    view = lambda t: t.reshape(N_SLABS, bsz, seq, LANES)
    spec = pl.BlockSpec((N_SLABS, None, seq, LANES), lambda b: (0, b, 0, 0))
    o, lse = pl.pallas_call(
        functools.partial(_attn_kernel, dilation=dilation, n_blocks=seq // dilation // ATT_BLOCK),
        grid=(bsz,),
        in_specs=[spec, spec, spec],
        out_specs=[spec, pl.BlockSpec((None, seq, LANES), lambda b: (b, 0, 0))],
        out_shape=[jax.ShapeDtypeStruct((N_SLABS, bsz, seq, LANES), F32),
                   jax.ShapeDtypeStruct((bsz, seq, LANES), F32)],
        compiler_params=_cparams("parallel"),
        name=f"dilated_attention_d{dilation}",
    )(view(q), view(k), view(v))
    return o.reshape(N_SLABS, bsz * seq, LANES), lse.reshape(bsz * seq, LANES)


def _merge_kernel(x_ref, ys_ref, o0_ref, o1_ref, o2_ref, l0_ref, l1_ref, l2_ref, cv_ref, halo_ref, gate_ref,
                  bgate_ref, wglu_ref, watt_ref, convw_ref, convb_ref, lng_ref, lnb_ref, wpw2_ref, wout_ref,
                  out_ref, conv_scr, *, tiles_per_seq):
    tm = x_ref.shape[0]
    z = jnp.dot(ys_ref[...].astype(BF16), wglu_ref[...], preferred_element_type=F32)
    y_s = z[:, :D_MODEL] * _sigmoid(z[:, D_MODEL:])

    lses = [l0_ref[...], l1_ref[...], l2_ref[...]]
    m = jnp.maximum(jnp.maximum(lses[0], lses[1]), lses[2])
    es = [jnp.exp(l - m) for l in lses]
    inv = 1.0 / (es[0] + es[1] + es[2])
    ws = [e * inv for e in es]
    lane_lo = lax.broadcasted_iota(jnp.int32, (tm, LANES), 1) < HEAD_DIM
    o_refs = (o0_ref, o1_ref, o2_ref)
    pieces = []
    for hp in range(N_SLABS):
        acc = jnp.zeros((tm, LANES), F32)
        for p in range(3):
            wexp = jnp.where(lane_lo, ws[p][:, 2 * hp:2 * hp + 1], ws[p][:, 2 * hp + 1:2 * hp + 2])
            acc = acc + wexp * o_refs[p][hp]
        pieces.append(acc.astype(BF16))
    o = jnp.concatenate(pieces, axis=1)
    y_a = jnp.dot(o, watt_ref[...], preferred_element_type=F32)

    def glu(t):
        t = t.astype(F32)
        return t[:, :BRANCH_W] * _sigmoid(t[:, BRANCH_W:])
    first = (pl.program_id(0) % tiles_per_seq) == 0
    halo = glu(halo_ref[...])
    conv_scr[0:CONV_HALO, :] = jnp.where(first, jnp.zeros_like(halo), halo)
    conv_scr[CONV_HALO:CONV_HALO + tm, :] = glu(cv_ref[...])
    acc = jnp.zeros((tm, BRANCH_W), F32) + convb_ref[...]
    base = CONV_HALO - (CONV_WIDTH - 1)
    for tap in range(CONV_WIDTH):
        acc = acc + convw_ref[tap:tap + 1, :] * conv_scr[base + tap:base + tap + tm, :]
    mu = jnp.mean(acc, axis=-1, keepdims=True)
    cen = acc - mu
    var = jnp.mean(cen * cen, axis=-1, keepdims=True)
    hn = cen * lax.rsqrt(var + EPS) * lng_ref[...] + lnb_ref[...]
    hc = (hn * _sigmoid(hn)).astype(BF16)
    y_c = jnp.dot(hc, wpw2_ref[...], preferred_element_type=F32)

    def gate(i):
        return _sigmoid(gate_ref[:, i * D_MODEL:(i + 1) * D_MODEL].astype(F32)
                        + bgate_ref[:, i * D_MODEL:(i + 1) * D_MODEL])
    merged = gate(0) * y_s + gate(1) * y_a + gate(2) * y_c
    out_ref[...] = x_ref[...] + jnp.dot(merged.astype(BF16), wout_ref[...], preferred_element_type=F32)


def merge_branches(x2, ys, os_, lses, cv, gates, b_gate, w_glu, w_att, conv_w, conv_b, ln_g, ln_b,
                   w_pw2, w_out, seq):
    n = x2.shape[0]
    tm = TM_MERGE
    row = lambda w: pl.BlockSpec((tm, w), lambda i: (i, 0))
    slab = pl.BlockSpec((N_SLABS, tm, LANES), lambda i: (0, i, 0))
    halo_blocks = tm // CONV_HALO
    halo_spec = pl.BlockSpec((CONV_HALO, 2 * BRANCH_W), lambda i: (jnp.maximum(i * halo_blocks - 1, 0), 0))
    return pl.pallas_call(
        functools.partial(_merge_kernel, tiles_per_seq=seq // tm),
        grid=(n // tm,),
        in_specs=[row(D_MODEL), row(BRANCH_W), slab, slab, slab,
                  row(LANES), row(LANES), row(LANES),
                  row(2 * BRANCH_W), halo_spec, row(3 * D_MODEL),
                  _const_spec((1, 3 * D_MODEL)), _const_spec((BRANCH_W, 2 * D_MODEL)),
                  _const_spec((BRANCH_W, D_MODEL)), _const_spec((CONV_WIDTH, BRANCH_W)),
                  _const_spec((1, BRANCH_W)), _const_spec((1, BRANCH_W)), _const_spec((1, BRANCH_W)),
                  _const_spec((BRANCH_W, D_MODEL)), _const_spec((D_MODEL, D_MODEL))],
        out_specs=row(D_MODEL),
        out_shape=jax.ShapeDtypeStruct((n, D_MODEL), F32),
        scratch_shapes=[pltpu.VMEM((CONV_HALO + tm, BRANCH_W), F32)],
        compiler_params=_cparams("parallel"),
        name="merge_branches",
    )(x2, ys, os_[0], os_[1], os_[2], lses[0], lses[1], lses[2], cv, cv, gates,
      b_gate.reshape(1, -1).astype(F32), w_glu, w_att, conv_w.astype(F32),
      conv_b.reshape(1, -1).astype(F32), ln_g.reshape(1, -1).astype(F32), ln_b.reshape(1, -1).astype(F32),
      w_pw2, w_out)


def _ffn_kernel(x_ref, g_ref, w1_ref, w2_ref, fg_ref, out_ref, *, final_norm):
    x = x_ref[...]
    h = _rms_norm(x, g_ref[...]).astype(BF16)
    acc = x
    for c0, c1 in zip(FF_SPLITS[:-1], FF_SPLITS[1:]):
        za = jnp.dot(h, w1_ref[:, c0:c1], preferred_element_type=F32)
        zb = jnp.dot(h, w1_ref[:, D_FF + c0:D_FF + c1], preferred_element_type=F32)
        act = (za * _sigmoid(za) * zb).astype(BF16)
        acc = acc + jnp.dot(act, w2_ref[c0:c1, :], preferred_element_type=F32)
    if final_norm:
        acc = _rms_norm(acc, fg_ref[...])
    out_ref[...] = acc


def ffn(x2, g, w1, w2, final_g, final_norm):
    n = x2.shape[0]
    tm = TM_FFN
    return pl.pallas_call(
        functools.partial(_ffn_kernel, final_norm=final_norm),
        grid=(n // tm,),
        in_specs=[pl.BlockSpec((tm, D_MODEL), lambda i: (i, 0)), _const_spec((1, D_MODEL)),
                  _const_spec((D_MODEL, 2 * D_FF)), _const_spec((D_FF, D_MODEL)), _const_spec((1, D_MODEL))],
        out_specs=pl.BlockSpec((tm, D_MODEL), lambda i: (i, 0)),
        out_shape=jax.ShapeDtypeStruct((n, D_MODEL), F32),
        compiler_params=_cparams("parallel"),
        name="ffn",
    )(x2, g.reshape(1, D_MODEL), w1, w2, final_g.reshape(1, D_MODEL))


def kernel(x, norm1_g, w_in, b_gate, ssm_lambda_re, ssm_lambda_im, ssm_log_dt, ssm_b_re, ssm_b_im, ssm_c_re,
           ssm_c_im, ssm_d, w_ssm_glu, w_att_up, conv_w, conv_b, conv_ln_g, conv_ln_b, w_conv_pw2, w_out,
           norm2_g, w_ffn_in, w_ffn_out, final_g):
    bsz, seq, _ = x.shape
    assert seq % (max(DILATIONS) * ATT_BLOCK) == 0 and seq % TM_MERGE == 0 and bsz % SSM_BATCH == 0
    x2 = x.reshape(bsz * seq, D_MODEL).astype(F32)
    for l in range(DEPTH):
        u, q0, q1, q2, k, v, cv, gates = in_projection(x2, norm1_g[l], w_in[l].astype(BF16))
        mats = ssm_matrices(ssm_lambda_re[l], ssm_lambda_im[l], ssm_log_dt[l], ssm_b_re[l], ssm_b_im[l],
                            ssm_c_re[l], ssm_c_im[l], ssm_d[l])
        ys = ssm_mixer(u, mats, bsz, seq)
        os_, lses = [], []
        for q, dil in zip((q0, q1, q2), DILATIONS):
            o, lse = dilated_attention_pattern(q, k, v, bsz, seq, dil)
            os_.append(o)
            lses.append(lse)
        x2 = merge_branches(x2, ys, os_, lses, cv, gates, b_gate[l], w_ssm_glu[l].astype(BF16),
                            w_att_up[l].astype(BF16), conv_w[l], conv_b[l], conv_ln_g[l], conv_ln_b[l],
                            w_conv_pw2[l].astype(BF16), w_out[l].astype(BF16), seq)
        x2 = ffn(x2, norm2_g[l], w_ffn_in[l].astype(BF16), w_ffn_out[l].astype(BF16), final_g,
                 final_norm=(l == DEPTH - 1))
    return x2.reshape(bsz, seq, D_MODEL)
```

```python
import functools
import math

import jax
import jax.numpy as jnp
from jax import lax
from jax.experimental import pallas as pl
from jax.experimental.pallas import tpu as pltpu

F32 = jnp.float32
BF16 = jnp.bfloat16

D_MODEL = 1024
DEPTH = 2
EPS = 1e-6
BRANCH_W = 512
SSM_GROUP = 16
SSM_GROUPS = 32
SSM_STATE = 64
HEAD_DIM = 64
ATT_HEADS = 8
DILATIONS = (1, 4, 16)
ATT_BLOCK = 128
ATT_SCALE = HEAD_DIM ** -0.5
CONV_WIDTH = 31
D_FF = 2816
FF_SPLITS = (0, 1536, 2816)
SSM_CHUNK = 16
CHUNK_W = SSM_CHUNK * SSM_GROUP
STATE_W = 2 * SSM_STATE
LANES = 128
CONV_HALO = 32
CONV_ROWS = 64

VMEM_LIMIT = 56 * 1024 * 1024

TM_PROJ = 512
TM_MERGE = 512
TM_FFN = 512


def _cparams(*sem):
    return pltpu.CompilerParams(dimension_semantics=sem, vmem_limit_bytes=VMEM_LIMIT)


def _const_spec(shape):
    nd = len(shape)
    return pl.BlockSpec(shape, lambda *_: (0,) * nd)


def _rms_norm(x, g):
    return x * lax.rsqrt(jnp.mean(x * x, axis=-1, keepdims=True) + EPS) * g


def _sigmoid(x):
    return 0.5 * jnp.tanh(0.5 * x) + 0.5


def _gelu_tanh(x):
    c = math.sqrt(2.0 / math.pi)
    return 0.5 * x * (1.0 + jnp.tanh(c * (x + 0.044715 * (x * x * x))))


N_SLABS = BRANCH_W // LANES
STRIPS = LANES // SSM_GROUP


def _strip_id(rows):
    return lax.broadcasted_iota(jnp.int32, (rows, LANES), 1) // SSM_GROUP


def _gather_strips(srcs, dst_strip_of_src, src_strip, strip_id):
    acc = None
    for src, ds, ss in zip(srcs, dst_strip_of_src, src_strip):
        shift = ((ds - ss) * SSM_GROUP) % LANES
        moved = src if shift == 0 else pltpu.roll(src, shift, axis=1)
        acc = moved if acc is None else jnp.where(strip_id == ds, moved, acc)
    return acc


def _inproj_kernel(x_ref, g_ref, w_ref, convw_ref, convb_ref, lng_ref, lnb_ref, bgate_ref,
                   u_ref, q0_ref, q1_ref, q2_ref, k_ref, v_ref, hc_ref, gate_ref, u_scr, conv_scr, cout_scr,
                   *, tiles_per_seq):
    tm = x_ref.shape[0]
    h = _rms_norm(x_ref[...], g_ref[...]).astype(BF16)

    def proj(col):
        return jnp.dot(h, w_ref[:, col:col + BRANCH_W], preferred_element_type=F32)

    cv_col = 5 * BRANCH_W + BRANCH_W
    glu = proj(cv_col) * _sigmoid(proj(cv_col + BRANCH_W))
    @pl.when(pl.program_id(0) == 0)
    def _():
        conv_scr[...] = jnp.zeros_like(conv_scr)

    first = (pl.program_id(0) % tiles_per_seq) == 0
    for sl in range(N_SLABS):
        prev = conv_scr[sl, pl.ds(2 * tm, CONV_HALO, stride=2), :]
        conv_scr[sl, pl.ds(0, CONV_HALO, stride=2), :] = jnp.where(first, jnp.zeros_like(prev), prev)
        conv_scr[sl, pl.ds(2 * CONV_HALO, tm, stride=2), :] = glu[:, sl * LANES:(sl + 1) * LANES]

    def conv_slab(sl):
        lanes = slice(sl * LANES, (sl + 1) * LANES)
        for r0 in range(0, tm, CONV_ROWS):
            acc = jnp.broadcast_to(convb_ref[:, lanes], (CONV_ROWS, LANES))
            for tap in range(CONV_WIDTH):
                row0 = 2 * (CONV_HALO - (CONV_WIDTH - 1) + tap + r0)
                acc = acc + convw_ref[tap:tap + 1, lanes] * conv_scr[sl, pl.ds(row0, CONV_ROWS, stride=2), :]
            cout_scr[r0:r0 + CONV_ROWS, lanes] = acc

    def attn_operand(ref, col, scale):
        acc = proj(col)
        if scale is not None:
            acc = acc * scale
        for sl in range(N_SLABS):
            ref[sl] = acc[:, sl * LANES:(sl + 1) * LANES]

    conv_slab(0)
    acc = proj(0)
    for sl in range(N_SLABS):
        u_scr[sl] = acc[:, sl * LANES:(sl + 1) * LANES]
    n_rows = tm // SSM_CHUNK
    strip_id = _strip_id(n_rows)
    for sl in range(N_SLABS):
        toks = [u_scr[sl, pl.ds(j, n_rows, stride=SSM_CHUNK), :] for j in range(SSM_CHUNK)]
        for gl in range(STRIPS):
            for jo in range(SSM_CHUNK // STRIPS):
                srcs = toks[jo * STRIPS:(jo + 1) * STRIPS]
                tile = _gather_strips(srcs, list(range(STRIPS)), [gl] * STRIPS, strip_id)
                lane0 = (sl * STRIPS + gl) * CHUNK_W + jo * LANES
                u_ref[:, lane0:lane0 + LANES] = tile.astype(BF16)
    attn_operand(q0_ref, 1 * BRANCH_W, ATT_SCALE)
    conv_slab(1)
    attn_operand(q1_ref, 2 * BRANCH_W, ATT_SCALE)
    conv_slab(2)
    attn_operand(q2_ref, 3 * BRANCH_W, ATT_SCALE)
    conv_slab(3)
    attn_operand(k_ref, 4 * BRANCH_W, None)
    attn_operand(v_ref, 5 * BRANCH_W, None)

    conv = cout_scr[...]
    mu = jnp.mean(conv, axis=-1, keepdims=True)
    cen = conv - mu
    var = jnp.mean(cen * cen, axis=-1, keepdims=True)
    hn = cen * lax.rsqrt(var + EPS) * lng_ref[...] + lnb_ref[...]
    hc_ref[...] = (hn * _sigmoid(hn)).astype(BF16)

    col = cv_col + 2 * BRANCH_W
    for c0 in range(0, gate_ref.shape[-1], BRANCH_W):
        gate_ref[:, c0:c0 + BRANCH_W] = _sigmoid(proj(col + c0) + bgate_ref[:, c0:c0 + BRANCH_W]).astype(BF16)


def in_projection(x2, g, w_bf16, conv_w, conv_b, ln_g, ln_b, b_gate, seq):
    n = x2.shape[0]
    tm = TM_PROJ
    in_cols = w_bf16.shape[1]
    slab_spec = pl.BlockSpec((N_SLABS, tm, LANES), lambda i: (0, i, 0))
    slab_shape = jax.ShapeDtypeStruct((N_SLABS, n, LANES), F32)
    u_w = SSM_GROUPS * CHUNK_W
    vec = lambda t: t.reshape(1, -1).astype(F32)
    return pl.pallas_call(
        functools.partial(_inproj_kernel, tiles_per_seq=seq // tm),
        grid=(n // tm,),
        in_specs=[pl.BlockSpec((tm, D_MODEL), lambda i: (i, 0)),
                  _const_spec((1, D_MODEL)),
                  _const_spec((D_MODEL, in_cols)),
                  _const_spec((CONV_WIDTH, BRANCH_W)), _const_spec((1, BRANCH_W)),
                  _const_spec((1, BRANCH_W)), _const_spec((1, BRANCH_W)), _const_spec((1, 3 * D_MODEL))],
        out_specs=[pl.BlockSpec((tm // SSM_CHUNK, u_w), lambda i: (i, 0))] + [slab_spec] * 5
                  + [pl.BlockSpec((tm, BRANCH_W), lambda i: (i, 0)),
                     pl.BlockSpec((tm, 3 * D_MODEL), lambda i: (i, 0))],
        out_shape=[jax.ShapeDtypeStruct((n // SSM_CHUNK, u_w), BF16)] + [slab_shape] * 5
                  + [jax.ShapeDtypeStruct((n, BRANCH_W), BF16), jax.ShapeDtypeStruct((n, 3 * D_MODEL), BF16)],
        scratch_shapes=[pltpu.VMEM((N_SLABS, tm, LANES), F32),
                        pltpu.VMEM((N_SLABS, 2 * (CONV_HALO + tm), LANES), F32),
                        pltpu.VMEM((tm, BRANCH_W), F32)],
        compiler_params=_cparams("arbitrary"),
        name="in_projection",
    )(x2, g.reshape(1, D_MODEL), w_bf16, conv_w.astype(F32), vec(conv_b), vec(ln_g), vec(ln_b), vec(b_gate))


def ssm_matrices(lam_re, lam_im, log_dt, b_re, b_im, c_re, c_im, d_skip):
    hi = lax.Precision.HIGHEST
    t = SSM_CHUNK
    dt = jnp.exp(log_dt.astype(F32))[:, None]
    lam_re = lam_re.astype(F32)
    lam_im = lam_im.astype(F32)
    mag = jnp.exp(lam_re * dt)
    ab_re = mag * jnp.cos(lam_im * dt)
    ab_im = mag * jnp.sin(lam_im * dt)
    nr, ni = ab_re - 1.0, ab_im
    den = lam_re * lam_re + lam_im * lam_im
    z_re = ((nr * lam_re + ni * lam_im) / den)[..., None]
    z_im = ((ni * lam_re - nr * lam_im) / den)[..., None]
    b_re = b_re.astype(F32)
    b_im = b_im.astype(F32)
    bb_re = z_re * b_re - z_im * b_im
    bb_im = z_re * b_im + z_im * b_re
    n = jnp.arange(t + 1, dtype=F32)[:, None, None]
    pmag = jnp.exp(n * (lam_re * dt)[None])
    pw_re = pmag * jnp.cos(n * (lam_im * dt)[None])
    pw_im = pmag * jnp.sin(n * (lam_im * dt)[None])
    c_re = c_re.astype(F32)
    c_im = c_im.astype(F32)
    ca_re = c_re[None] * pw_re[:, :, None, :] - c_im[None] * pw_im[:, :, None, :]
    ca_im = c_re[None] * pw_im[:, :, None, :] + c_im[None] * pw_re[:, :, None, :]
    kmat = (jnp.einsum('dgcp,gph->dgch', ca_re[:t], bb_re, precision=hi)
            - jnp.einsum('dgcp,gph->dgch', ca_im[:t], bb_im, precision=hi))
    eye = jnp.eye(SSM_GROUP, dtype=F32)
    kmat = kmat.at[0].add(d_skip.astype(F32).reshape(SSM_GROUPS, SSM_GROUP)[:, :, None] * eye[None])
    kext = jnp.concatenate([kmat, jnp.zeros_like(kmat[:1])], axis=0)
    jj = jnp.arange(t)
    lag = jj[None, :] - jj[:, None]
    lag = jnp.where(lag >= 0, lag, t)
    m5 = kext[lag]
    m_toep = m5.transpose(2, 0, 4, 1, 3).reshape(SSM_GROUPS, CHUNK_W, CHUNK_W)
    rev_re = pw_re[:t][::-1]
    rev_im = pw_im[:t][::-1]
    wb_re = rev_re[..., None] * bb_re[None] - rev_im[..., None] * bb_im[None]
    wb_im = rev_re[..., None] * bb_im[None] + rev_im[..., None] * bb_re[None]
    wb = jnp.concatenate([wb_re, wb_im], axis=2)
    w_b = wb.transpose(1, 0, 3, 2).reshape(SSM_GROUPS, CHUNK_W, STATE_W)
    wc = jnp.concatenate([ca_re[1:], -ca_im[1:]], axis=3)
    w_c = wc.transpose(1, 3, 0, 2).reshape(SSM_GROUPS, STATE_W, CHUNK_W)
    a1 = jnp.concatenate([pw_re[t], pw_re[t]], axis=-1)[:, None, :]
    a2 = jnp.concatenate([-pw_im[t], pw_im[t]], axis=-1)[:, None, :]
    return m_toep.astype(BF16), w_b.astype(BF16), w_c.astype(BF16), a1, a2


SSM_BATCH = 8
SSM_PITCH = 136
SSM_OUT_ROWS = 512


def _ssm_kernel(u_ref, m_ref, wb_ref, wc_ref, a1_ref, a2_ref, y_ref, x_scr, sp_scr, *, n_chunks):
    nb = SSM_BATCH
    rows = nb * n_chunks
    for g in range(STRIPS):
        x = jnp.dot(u_ref[:, g * CHUNK_W:(g + 1) * CHUNK_W], wb_ref[g], preferred_element_type=F32)
        for b in range(nb):
            x_scr[g, b * SSM_PITCH:b * SSM_PITCH + n_chunks, :] = x[b * n_chunks:(b + 1) * n_chunks]
    a1 = [jnp.broadcast_to(a1_ref[g], (nb, STATE_W)) for g in range(STRIPS)]
    a2 = [jnp.broadcast_to(a2_ref[g], (nb, STATE_W)) for g in range(STRIPS)]

    def step(s, states):
        new = []
        for g in range(STRIPS):
            st = states[g]
            sp_scr[g, pl.ds(s, nb, stride=n_chunks), :] = st
            swapped = pltpu.roll(st, SSM_STATE, axis=1)
            new.append(a1[g] * st + a2[g] * swapped + x_scr[g, pl.ds(s, nb, stride=SSM_PITCH), :])
        return tuple(new)

    lax.fori_loop(0, n_chunks, step, tuple(jnp.zeros((nb, STATE_W), F32) for _ in range(STRIPS)))

    rc = SSM_OUT_ROWS
    for r0 in range(0, rows, rc):
        for g in range(STRIPS):
            lanes = slice(g * CHUNK_W, (g + 1) * CHUNK_W)
            y = (jnp.dot(u_ref[r0:r0 + rc, lanes], m_ref[g], preferred_element_type=F32)
                 + jnp.dot(sp_scr[g, r0:r0 + rc, :].astype(BF16), wc_ref[g], preferred_element_type=F32))
            y_ref[r0:r0 + rc, lanes] = _gelu_tanh(y).astype(BF16)


def ssm_mixer(u_chunks, mats, bsz, seq):
    m_toep, w_b, w_c, a1, a2 = mats
    ns = seq // SSM_CHUNK
    nb = SSM_BATCH
    gw = STRIPS * CHUNK_W
    gspec = lambda shape: pl.BlockSpec((STRIPS,) + shape, lambda i, j: (j, 0, 0))
    chunk_spec = pl.BlockSpec((nb * ns, gw), lambda i, j: (i, j))
    return pl.pallas_call(
        functools.partial(_ssm_kernel, n_chunks=ns),
        grid=(bsz // nb, SSM_GROUPS // STRIPS),
        in_specs=[chunk_spec, gspec((CHUNK_W, CHUNK_W)), gspec((CHUNK_W, STATE_W)),
                  gspec((STATE_W, CHUNK_W)), gspec((1, STATE_W)), gspec((1, STATE_W))],
        out_specs=chunk_spec,
        out_shape=jax.ShapeDtypeStruct(u_chunks.shape, BF16),
        scratch_shapes=[pltpu.VMEM((STRIPS, nb * SSM_PITCH, STATE_W), F32),
                        pltpu.VMEM((STRIPS, nb * ns, STATE_W), F32)],
        compiler_params=_cparams("parallel", "parallel"),
        name="ssm_mixer",
    )(u_chunks, m_toep, w_b, w_c, a1, a2)


def _attn_scores(q2, kc, lane_lo):
    zero = jnp.zeros_like(q2)
    qq = jnp.concatenate([jnp.where(lane_lo, q2, zero), jnp.where(lane_lo, zero, q2)], axis=0)
    return lax.dot_general(qq, kc, (((1,), (1,)), ((), ())), preferred_element_type=F32)


def _attn_softmax(s, valid):
    ps, dens, lses = [], [], []
    for half in range(2):
        sh = jnp.where(valid, s[half * ATT_BLOCK:(half + 1) * ATT_BLOCK], -jnp.inf)
        m = jnp.max(sh, axis=1, keepdims=True)
        p = jnp.exp(sh - m)
        den = jnp.sum(p, axis=1, keepdims=True)
        ps.append(p.astype(BF16))
        dens.append(den)
        lses.append(m + jnp.log(den))
    return jnp.concatenate(ps, axis=0), dens, lses


def _attn_kernel(q_ref, k_ref, v_ref, o_ref, lse_ref, *, dilation, n_blocks):
    blk = ATT_BLOCK
    lane = lax.broadcasted_iota(jnp.int32, (blk, LANES), 1)
    lane_lo = lane < HEAD_DIM
    qi = lax.broadcasted_iota(jnp.int32, (blk, 2 * blk), 0)
    kj = lax.broadcasted_iota(jnp.int32, (blk, 2 * blk), 1)
    dist = qi - kj + blk
    valid2 = (dist >= 0) & (dist <= blk)
    valid1 = (lax.broadcasted_iota(jnp.int32, (blk, blk), 1)
              <= lax.broadcasted_iota(jnp.int32, (blk, blk), 0))
    stride = None if dilation == 1 else dilation

    def rows(start, n):
        return pl.ds(start, n) if stride is None else pl.ds(start, n, stride=stride)

    def do_block(start, first):
        valid = valid1 if first else valid2
        prev = start - blk * dilation

        def keys(ref, hp):
            cur = ref[hp, rows(start, blk), :].astype(BF16)
            if first:
                return cur
            return jnp.concatenate([ref[hp, rows(prev, blk), :].astype(BF16), cur], axis=0)

        scores = [_attn_scores(q_ref[hp, rows(start, blk), :].astype(BF16), keys(k_ref, hp), lane_lo)
                  for hp in range(N_SLABS)]
        soft = [_attn_softmax(s, valid) for s in scores]
        lse_rows = jnp.zeros((blk, LANES), F32)
        for hp, (p, dens, lses) in enumerate(soft):
            pv = jnp.dot(p, keys(v_ref, hp), preferred_element_type=F32)
            o_ref[hp, rows(start, blk), :] = jnp.where(lane_lo, pv[:blk] / dens[0], pv[blk:] / dens[1])
            lse_rows = jnp.where(lane == 2 * hp, lses[0], lse_rows)
            lse_rows = jnp.where(lane == 2 * hp + 1, lses[1], lse_rows)
        lse_ref[rows(start, blk), :] = lse_rows

    for r in range(dilation):
        do_block(r, True)
        if n_blocks > 1:
            def body(n, carry):
                do_block(r + n * (blk * dilation), False)
                return carry
            lax.fori_loop(1, n_blocks, body, 0)


def dilated_attention_pattern(q, k, v, bsz, seq, dilation):
    view = lambda t: t.reshape(N_SLABS, bsz, seq, LANES)
    spec = pl.BlockSpec((N_SLABS, None, seq, LANES), lambda b: (0, b, 0, 0))
    o, lse = pl.pallas_call(
        functools.partial(_attn_kernel, dilation=dilation, n_blocks=seq // dilation // ATT_BLOCK),
        grid=(bsz,),
        in_specs=[spec, spec, spec],
        out_specs=[spec, pl.BlockSpec((None, seq, LANES), lambda b: (b, 0, 0))],
        out_shape=[jax.ShapeDtypeStruct((N_SLABS, bsz, seq, LANES), F32),
                   jax.ShapeDtypeStruct((bsz, seq, LANES), F32)],
        compiler_params=_cparams("parallel"),
        name=f"dilated_attention_d{dilation}",
    )(view(q), view(k), view(v))
    return o.reshape(N_SLABS, bsz * seq, LANES), lse.reshape(bsz * seq, LANES)


def _merge_kernel(x_ref, ys_ref, o0_ref, o1_ref, o2_ref, l0_ref, l1_ref, l2_ref, hc_ref, gate_ref,
                  wglu_ref, watt_ref, wpw2_ref, wout_ref, out_ref, ys_scr):
    tm = x_ref.shape[0]

    def gate(i):
        return gate_ref[:, i * D_MODEL:(i + 1) * D_MODEL].astype(F32)

    merged = gate(2) * jnp.dot(hc_ref[...], wpw2_ref[...], preferred_element_type=F32)

    n_rows = tm // SSM_CHUNK
    strip_id = _strip_id(n_rows)
    for sl in range(N_SLABS):
        halves = [[ys_ref[:, (sl * STRIPS + gl) * CHUNK_W + jo * LANES:
                          (sl * STRIPS + gl) * CHUNK_W + (jo + 1) * LANES].astype(F32)
                   for gl in range(STRIPS)] for jo in range(SSM_CHUNK // STRIPS)]
        for j in range(SSM_CHUNK):
            jo, k = divmod(j, STRIPS)
            tile = _gather_strips(halves[jo], list(range(STRIPS)), [k] * STRIPS, strip_id)
            ys_scr[sl, pl.ds(j, n_rows, stride=SSM_CHUNK), :] = tile
    ys = jnp.concatenate([ys_scr[sl] for sl in range(N_SLABS)], axis=1).astype(BF16)
    z = jnp.dot(ys, wglu_ref[...], preferred_element_type=F32)

    lses = [l0_ref[...], l1_ref[...], l2_ref[...]]
    m = jnp.maximum(jnp.maximum(lses[0], lses[1]), lses[2])
    es = [jnp.exp(l - m) for l in lses]
    inv = 1.0 / (es[0] + es[1] + es[2])
    ws = [e * inv for e in es]
    lane_lo = lax.broadcasted_iota(jnp.int32, (tm, LANES), 1) < HEAD_DIM
    o_refs = (o0_ref, o1_ref, o2_ref)
    pieces = []
    for hp in range(N_SLABS):
        acc = jnp.zeros((tm, LANES), F32)
        for p in range(3):
            wexp = jnp.where(lane_lo, ws[p][:, 2 * hp:2 * hp + 1], ws[p][:, 2 * hp + 1:2 * hp + 2])
            acc = acc + wexp * o_refs[p][hp]
        pieces.append(acc.astype(BF16))
    o = jnp.concatenate(pieces, axis=1)
    y_a = jnp.dot(o, watt_ref[...], preferred_element_type=F32)

    y_s = z[:, :D_MODEL] * _sigmoid(z[:, D_MODEL:])
    merged = gate(0) * y_s + gate(1) * y_a + merged
    out_ref[...] = x_ref[...] + jnp.dot(merged.astype(BF16), wout_ref[...], preferred_element_type=F32)


def merge_branches(x2, ys_chunks, os_, lses, hc, gates, w_glu, w_att, w_pw2, w_out):
    n = x2.shape[0]
    tm = TM_MERGE
    row = lambda w: pl.BlockSpec((tm, w), lambda i: (i, 0))
    slab = pl.BlockSpec((N_SLABS, tm, LANES), lambda i: (0, i, 0))
    return pl.pallas_call(
        _merge_kernel,
        grid=(n // tm,),
        in_specs=[row(D_MODEL), pl.BlockSpec((tm // SSM_CHUNK, SSM_GROUPS * CHUNK_W), lambda i: (i, 0)),
                  slab, slab, slab, row(LANES), row(LANES), row(LANES),
                  row(BRANCH_W), row(3 * D_MODEL),
                  _const_spec((BRANCH_W, 2 * D_MODEL)), _const_spec((BRANCH_W, D_MODEL)),
                  _const_spec((BRANCH_W, D_MODEL)), _const_spec((D_MODEL, D_MODEL))],
        out_specs=row(D_MODEL),
        out_shape=jax.ShapeDtypeStruct((n, D_MODEL), F32),
        scratch_shapes=[pltpu.VMEM((N_SLABS, tm, LANES), F32)],
        compiler_params=_cparams("parallel"),
        name="merge_branches",
    )(x2, ys_chunks, os_[0], os_[1], os_[2], lses[0], lses[1], lses[2], hc, gates, w_glu, w_att, w_pw2, w_out)


def _ffn_kernel(x_ref, g_ref, w1_ref, w2_ref, fg_ref, out_ref, *, final_norm):
    x = x_ref[...]
    h = _rms_norm(x, g_ref[...]).astype(BF16)
    acc = x
    for c0, c1 in zip(FF_SPLITS[:-1], FF_SPLITS[1:]):
        za = jnp.dot(h, w1_ref[:, c0:c1], preferred_element_type=F32)
        zb = jnp.dot(h, w1_ref[:, D_FF + c0:D_FF + c1], preferred_element_type=F32)
        act = (za * _sigmoid(za) * zb).astype(BF16)
        acc = acc + jnp.dot(act, w2_ref[c0:c1, :], preferred_element_type=F32)
    if final_norm:
        acc = _rms_norm(acc, fg_ref[...])
    out_ref[...] = acc


def ffn(x2, g, w1, w2, final_g, final_norm):
    n = x2.shape[0]
    tm = TM_FFN
    return pl.pallas_call(
        functools.partial(_ffn_kernel, final_norm=final_norm),
        grid=(n // tm,),
        in_specs=[pl.BlockSpec((tm, D_MODEL), lambda i: (i, 0)), _const_spec((1, D_MODEL)),
                  _const_spec((D_MODEL, 2 * D_FF)), _const_spec((D_FF, D_MODEL)), _const_spec((1, D_MODEL))],
        out_specs=pl.BlockSpec((tm, D_MODEL), lambda i: (i, 0)),
        out_shape=jax.ShapeDtypeStruct((n, D_MODEL), F32),
        compiler_params=_cparams("parallel"),
        name="ffn",
    )(x2, g.reshape(1, D_MODEL), w1, w2, final_g.reshape(1, D_MODEL))


def kernel(x, norm1_g, w_in, b_gate, ssm_lambda_re, ssm_lambda_im, ssm_log_dt, ssm_b_re, ssm_b_im, ssm_c_re,
           ssm_c_im, ssm_d, w_ssm_glu, w_att_up, conv_w, conv_b, conv_ln_g, conv_ln_b, w_conv_pw2, w_out,
           norm2_g, w_ffn_in, w_ffn_out, final_g):
    bsz, seq, _ = x.shape
    assert seq % (max(DILATIONS) * ATT_BLOCK) == 0 and seq % TM_MERGE == 0 and bsz % SSM_BATCH == 0
    x2 = x.reshape(bsz * seq, D_MODEL).astype(F32)
    for l in range(DEPTH):
        u, q0, q1, q2, k, v, hc, gates = in_projection(x2, norm1_g[l], w_in[l].astype(BF16), conv_w[l], conv_b[l],
                                                       conv_ln_g[l], conv_ln_b[l], b_gate[l], seq)
        mats = ssm_matrices(ssm_lambda_re[l], ssm_lambda_im[l], ssm_log_dt[l], ssm_b_re[l], ssm_b_im[l],
                            ssm_c_re[l], ssm_c_im[l], ssm_d[l])
        ys = ssm_mixer(u, mats, bsz, seq)
        os_, lses = [], []
        for q, dil in zip((q0, q1, q2), DILATIONS):
            o, lse = dilated_attention_pattern(q, k, v, bsz, seq, dil)
            os_.append(o)
            lses.append(lse)
        x2 = merge_branches(x2, ys, os_, lses, hc, gates, w_ssm_glu[l].astype(BF16), w_att_up[l].astype(BF16),
                            w_conv_pw2[l].astype(BF16), w_out[l].astype(BF16))
        x2 = ffn(x2, norm2_g[l], w_ffn_in[l].astype(BF16), w_ffn_out[l].astype(BF16), final_g,
                 final_norm=(l == DEPTH - 1))
    return x2.reshape(bsz, seq, D_MODEL)
```

```python
import functools
import math

import jax
import jax.numpy as jnp
from jax import lax
from jax.experimental import pallas as pl
from jax.experimental.pallas import tpu as pltpu

F32 = jnp.float32
BF16 = jnp.bfloat16

D_MODEL = 1024
DEPTH = 2
EPS = 1e-6
BRANCH_W = 512
SSM_GROUP = 16
SSM_GROUPS = 32
SSM_STATE = 64
HEAD_DIM = 64
ATT_HEADS = 8
DILATIONS = (1, 4, 16)
ATT_BLOCK = 128
ATT_SCALE = HEAD_DIM ** -0.5
CONV_WIDTH = 31
D_FF = 2816
FF_SPLITS = (0, 1536, 2816)
SSM_CHUNK = 16
CHUNK_W = SSM_CHUNK * SSM_GROUP
STATE_W = 2 * SSM_STATE
LANES = 128
SUBLANES = 8
CONV_HALO = 32
CONV_ROWS = 64

VMEM_LIMIT = 56 * 1024 * 1024

TM_PROJ = 512
PROJ_COLS = 256
VEC_JOBS_PER_MATMUL = 3
TM_MERGE = 512
TM_FFN = 512


def _cparams(*sem):
    return pltpu.CompilerParams(dimension_semantics=sem, vmem_limit_bytes=VMEM_LIMIT)


def _const_spec(shape):
    nd = len(shape)
    return pl.BlockSpec(shape, lambda *_: (0,) * nd)


def _rms_norm(x, g):
    return x * lax.rsqrt(jnp.mean(x * x, axis=-1, keepdims=True) + EPS) * g


def _sigmoid(x):
    return 0.5 * jnp.tanh(0.5 * x) + 0.5


def _gelu_tanh(x):
    c = math.sqrt(2.0 / math.pi)
    return 0.5 * x * (1.0 + jnp.tanh(c * (x + 0.044715 * (x * x * x))))


N_SLABS = BRANCH_W // LANES
STRIPS = LANES // SSM_GROUP


def _strip_id(rows):
    return lax.broadcasted_iota(jnp.int32, (rows, LANES), 1) // SSM_GROUP


def _gather_strips(srcs, dst_strip_of_src, src_strip, strip_id):
    acc = None
    for src, ds, ss in zip(srcs, dst_strip_of_src, src_strip):
        shift = ((ds - ss) * SSM_GROUP) % LANES
        moved = src if shift == 0 else pltpu.roll(src, shift, axis=1)
        acc = moved if acc is None else jnp.where(strip_id == ds, moved, acc)
    return acc


def _inproj_kernel(x_ref, g_ref, w_ref, convw_ref, convb_ref, lng_ref, lnb_ref, bgate_ref,
                   u_ref, q0_ref, q1_ref, q2_ref, k_ref, v_ref, hc_ref, gate_ref, u_scr, conv_scr, cout_scr,
                   *, tiles_per_seq):
    tm = x_ref.shape[0]
    h = _rms_norm(x_ref[...], g_ref[...]).astype(BF16)

    pw = PROJ_COLS

    def proj(col):
        return jnp.dot(h, w_ref[:, col:col + pw], preferred_element_type=F32)

    @pl.when(pl.program_id(0) == 0)
    def _():
        conv_scr[...] = jnp.zeros_like(conv_scr)

    cv_col = 6 * BRANCH_W
    first = (pl.program_id(0) % tiles_per_seq) == 0
    for half in range(BRANCH_W // pw):
        glu = proj(cv_col + half * pw) * _sigmoid(proj(cv_col + BRANCH_W + half * pw))
        for i in range(pw // LANES):
            sl = half * (pw // LANES) + i
            prev = conv_scr[sl, pl.ds(2 * tm, CONV_HALO, stride=2), :]
            conv_scr[sl, pl.ds(0, CONV_HALO, stride=2), :] = jnp.where(first, jnp.zeros_like(prev), prev)
            conv_scr[sl, pl.ds(2 * CONV_HALO, tm, stride=2), :] = glu[:, i * LANES:(i + 1) * LANES]

    def conv_chunk(sl, r0):
        lanes = slice(sl * LANES, (sl + 1) * LANES)
        acc = jnp.broadcast_to(convb_ref[:, lanes], (CONV_ROWS, LANES))
        for tap in range(CONV_WIDTH):
            row0 = 2 * (CONV_HALO - (CONV_WIDTH - 1) + tap + r0)
            acc = acc + convw_ref[tap:tap + 1, lanes] * conv_scr[sl, pl.ds(row0, CONV_ROWS, stride=2), :]
        cout_scr[r0:r0 + CONV_ROWS, lanes] = acc

    n_rows = tm // SSM_CHUNK
    strip_id = _strip_id(n_rows)

    def permute_u(sl):
        toks = [u_scr[sl, pl.ds(j, n_rows, stride=SSM_CHUNK), :] for j in range(SSM_CHUNK)]
        for gl in range(STRIPS):
            for jo in range(SSM_CHUNK // STRIPS):
                srcs = toks[jo * STRIPS:(jo + 1) * STRIPS]
                tile = _gather_strips(srcs, list(range(STRIPS)), [gl] * STRIPS, strip_id)
                lane0 = (sl * STRIPS + gl) * CHUNK_W + jo * LANES
                u_ref[:, lane0:lane0 + LANES] = tile.astype(BF16)

    vec_jobs = []
    for sl in range(N_SLABS):
        vec_jobs.append(functools.partial(permute_u, sl))
        vec_jobs += [functools.partial(conv_chunk, sl, r0) for r0 in range(0, tm, CONV_ROWS)]

    def run_vec(n):
        for _ in range(min(n, len(vec_jobs))):
            vec_jobs.pop(0)()

    def slab_store(ref, col0, acc):
        for i in range(pw // LANES):
            ref[(col0 % BRANCH_W) // LANES + i] = acc[:, i * LANES:(i + 1) * LANES]

    for col in range(0, BRANCH_W, pw):
        slab_store(u_scr, col, proj(col))
    run_vec(2)
    for idx, ref in enumerate((q0_ref, q1_ref, q2_ref, k_ref, v_ref)):
        for c0 in range(0, BRANCH_W, pw):
            acc = proj((idx + 1) * BRANCH_W + c0)
            if idx < 3:
                acc = acc * ATT_SCALE
            slab_store(ref, c0, acc)
            run_vec(VEC_JOBS_PER_MATMUL)

    gate_col = cv_col + 2 * BRANCH_W
    ln_done = False
    for c0 in range(0, gate_ref.shape[-1], pw):
        if not vec_jobs and not ln_done:
            conv = cout_scr[...]
            mu = jnp.mean(conv, axis=-1, keepdims=True)
            cen = conv - mu
            var = jnp.mean(cen * cen, axis=-1, keepdims=True)
            hn = cen * lax.rsqrt(var + EPS) * lng_ref[...] + lnb_ref[...]
            hc_ref[...] = (hn * _sigmoid(hn)).astype(BF16)
            ln_done = True
        gate_ref[:, c0:c0 + pw] = _sigmoid(proj(gate_col + c0) + bgate_ref[:, c0:c0 + pw]).astype(BF16)
        run_vec(VEC_JOBS_PER_MATMUL)
    assert ln_done and not vec_jobs


def in_projection(x2, g, w_bf16, conv_w, conv_b, ln_g, ln_b, b_gate, seq):
    n = x2.shape[0]
    tm = TM_PROJ
    in_cols = w_bf16.shape[1]
    slab_spec = pl.BlockSpec((N_SLABS, tm, LANES), lambda i: (0, i, 0))
    slab_shape = jax.ShapeDtypeStruct((N_SLABS, n, LANES), F32)
    u_w = SSM_GROUPS * CHUNK_W
    vec = lambda t: t.reshape(1, -1).astype(F32)
    return pl.pallas_call(
        functools.partial(_inproj_kernel, tiles_per_seq=seq // tm),
        grid=(n // tm,),
        in_specs=[pl.BlockSpec((tm, D_MODEL), lambda i: (i, 0)),
                  _const_spec((1, D_MODEL)),
                  pl.BlockSpec((D_MODEL, in_cols), lambda i: (0, 0), pipeline_mode=pl.Buffered(1)),
                  _const_spec((CONV_WIDTH, BRANCH_W)), _const_spec((1, BRANCH_W)),
                  _const_spec((1, BRANCH_W)), _const_spec((1, BRANCH_W)), _const_spec((1, 3 * D_MODEL))],
        out_specs=[pl.BlockSpec((tm // SSM_CHUNK, u_w), lambda i: (i, 0))] + [slab_spec] * 5
                  + [pl.BlockSpec((tm, BRANCH_W), lambda i: (i, 0)),
                     pl.BlockSpec((tm, 3 * D_MODEL), lambda i: (i, 0))],
        out_shape=[jax.ShapeDtypeStruct((n // SSM_CHUNK, u_w), BF16)] + [slab_shape] * 5
                  + [jax.ShapeDtypeStruct((n, BRANCH_W), BF16), jax.ShapeDtypeStruct((n, 3 * D_MODEL), BF16)],
        scratch_shapes=[pltpu.VMEM((N_SLABS, tm, LANES), F32),
                        pltpu.VMEM((N_SLABS, 2 * (CONV_HALO + tm), LANES), F32),
                        pltpu.VMEM((tm, BRANCH_W), F32)],
        compiler_params=_cparams("arbitrary"),
        name="in_projection",
    )(x2, g.reshape(1, D_MODEL), w_bf16, conv_w.astype(F32), vec(conv_b), vec(ln_g), vec(ln_b), vec(b_gate))


def ssm_matrices(lam_re, lam_im, log_dt, b_re, b_im, c_re, c_im, d_skip):
    hi = lax.Precision.HIGHEST
    t = SSM_CHUNK
    dt = jnp.exp(log_dt.astype(F32))[:, None]
    lam_re = lam_re.astype(F32)
    lam_im = lam_im.astype(F32)
    mag = jnp.exp(lam_re * dt)
    ab_re = mag * jnp.cos(lam_im * dt)
    ab_im = mag * jnp.sin(lam_im * dt)
    nr, ni = ab_re - 1.0, ab_im
    den = lam_re * lam_re + lam_im * lam_im
    z_re = ((nr * lam_re + ni * lam_im) / den)[..., None]
    z_im = ((ni * lam_re - nr * lam_im) / den)[..., None]
    b_re = b_re.astype(F32)
    b_im = b_im.astype(F32)
    bb_re = z_re * b_re - z_im * b_im
    bb_im = z_re * b_im + z_im * b_re
    n = jnp.arange(t + 1, dtype=F32)[:, None, None]
    pmag = jnp.exp(n * (lam_re * dt)[None])
    pw_re = pmag * jnp.cos(n * (lam_im * dt)[None])
    pw_im = pmag * jnp.sin(n * (lam_im * dt)[None])
    c_re = c_re.astype(F32)
    c_im = c_im.astype(F32)
    ca_re = c_re[None] * pw_re[:, :, None, :] - c_im[None] * pw_im[:, :, None, :]
    ca_im = c_re[None] * pw_im[:, :, None, :] + c_im[None] * pw_re[:, :, None, :]
    kmat = (jnp.einsum('dgcp,gph->dgch', ca_re[:t], bb_re, precision=hi)
            - jnp.einsum('dgcp,gph->dgch', ca_im[:t], bb_im, precision=hi))
    eye = jnp.eye(SSM_GROUP, dtype=F32)
    kmat = kmat.at[0].add(d_skip.astype(F32).reshape(SSM_GROUPS, SSM_GROUP)[:, :, None] * eye[None])
    kext = jnp.concatenate([kmat, jnp.zeros_like(kmat[:1])], axis=0)
    jj = jnp.arange(t)
    lag = jj[None, :] - jj[:, None]
    lag = jnp.where(lag >= 0, lag, t)
    m5 = kext[lag]
    m_toep = m5.transpose(2, 0, 4, 1, 3).reshape(SSM_GROUPS, CHUNK_W, CHUNK_W)
    rev_re = pw_re[:t][::-1]
    rev_im = pw_im[:t][::-1]
    wb_re = rev_re[..., None] * bb_re[None] - rev_im[..., None] * bb_im[None]
    wb_im = rev_re[..., None] * bb_im[None] + rev_im[..., None] * bb_re[None]
    wb = jnp.concatenate([wb_re, wb_im], axis=2)
    w_b = wb.transpose(1, 0, 3, 2).reshape(SSM_GROUPS, CHUNK_W, STATE_W)
    wc = jnp.concatenate([ca_re[1:], -ca_im[1:]], axis=3)
    w_c = wc.transpose(1, 3, 0, 2).reshape(SSM_GROUPS, STATE_W, CHUNK_W)
    a1 = jnp.concatenate([pw_re[t], pw_re[t]], axis=-1)[:, None, :]
    a2 = jnp.concatenate([-pw_im[t], pw_im[t]], axis=-1)[:, None, :]
    return m_toep.astype(BF16), w_b.astype(BF16), w_c.astype(BF16), a1, a2


SSM_BATCH = 8
SSM_PITCH = 136
SSM_OUT_ROWS = 512


def _ssm_kernel(u_ref, m_ref, wb_ref, wc_ref, a1_ref, a2_ref, y_ref, x_scr, xsw_scr, sp_scr, *, n_chunks):
    nb = SSM_BATCH
    rows = nb * n_chunks
    for g in range(STRIPS):
        x = jnp.dot(u_ref[:, g * CHUNK_W:(g + 1) * CHUNK_W], wb_ref[g], preferred_element_type=F32)
        xsw = pltpu.roll(x, SSM_STATE, axis=1)
        for b in range(nb):
            dst = slice(b * SSM_PITCH, b * SSM_PITCH + n_chunks)
            x_scr[g, dst, :] = x[b * n_chunks:(b + 1) * n_chunks]
            xsw_scr[g, dst, :] = xsw[b * n_chunks:(b + 1) * n_chunks]
    a1 = [jnp.broadcast_to(a1_ref[g], (nb, STATE_W)) for g in range(STRIPS)]
    a2 = [jnp.broadcast_to(a2_ref[g], (nb, STATE_W)) for g in range(STRIPS)]

    def step(s, states):
        new = []
        for g in range(STRIPS):
            st, sw = states[2 * g], states[2 * g + 1]
            sp_scr[g, pl.ds(s, nb, stride=n_chunks), :] = st
            chunk_rows = pl.ds(s, nb, stride=SSM_PITCH)
            new.append(a1[g] * st + a2[g] * sw + x_scr[g, chunk_rows, :])
            new.append(a1[g] * sw - a2[g] * st + xsw_scr[g, chunk_rows, :])
        return tuple(new)

    lax.fori_loop(0, n_chunks, step, tuple(jnp.zeros((nb, STATE_W), F32) for _ in range(2 * STRIPS)))

    rc = SSM_OUT_ROWS
    for r0 in range(0, rows, rc):
        for g in range(STRIPS):
            lanes = slice(g * CHUNK_W, (g + 1) * CHUNK_W)
            y = (jnp.dot(u_ref[r0:r0 + rc, lanes], m_ref[g], preferred_element_type=F32)
                 + jnp.dot(sp_scr[g, r0:r0 + rc, :].astype(BF16), wc_ref[g], preferred_element_type=F32))
            y_ref[r0:r0 + rc, lanes] = _gelu_tanh(y).astype(BF16)


def ssm_mixer(u_chunks, mats, bsz, seq):
    m_toep, w_b, w_c, a1, a2 = mats
    ns = seq // SSM_CHUNK
    nb = SSM_BATCH
    gw = STRIPS * CHUNK_W
    gspec = lambda shape: pl.BlockSpec((STRIPS,) + shape, lambda i, j: (j, 0, 0))
    chunk_spec = pl.BlockSpec((nb * ns, gw), lambda i, j: (i, j))
    return pl.pallas_call(
        functools.partial(_ssm_kernel, n_chunks=ns),
        grid=(bsz // nb, SSM_GROUPS // STRIPS),
        in_specs=[chunk_spec, gspec((CHUNK_W, CHUNK_W)), gspec((CHUNK_W, STATE_W)),
                  gspec((STATE_W, CHUNK_W)), gspec((1, STATE_W)), gspec((1, STATE_W))],
        out_specs=chunk_spec,
        out_shape=jax.ShapeDtypeStruct(u_chunks.shape, BF16),
        scratch_shapes=[pltpu.VMEM((STRIPS, nb * SSM_PITCH, STATE_W), F32),
                        pltpu.VMEM((STRIPS, nb * SSM_PITCH, STATE_W), F32),
                        pltpu.VMEM((STRIPS, nb * ns, STATE_W), F32)],
        compiler_params=_cparams("parallel", "parallel"),
        name="ssm_mixer",
    )(u_chunks, m_toep, w_b, w_c, a1, a2)


def _attn_scores(q2, kc, lane_lo):
    zero = jnp.zeros_like(q2)
    qq = jnp.concatenate([jnp.where(lane_lo, q2, zero), jnp.where(lane_lo, zero, q2)], axis=0)
    return lax.dot_general(qq, kc, (((1,), (1,)), ((), ())), preferred_element_type=F32)


def _attn_softmax(s, valid):
    ps, dens, lses = [], [], []
    for half in range(2):
        sh = jnp.where(valid, s[half * ATT_BLOCK:(half + 1) * ATT_BLOCK], -jnp.inf)
        m = jnp.max(sh, axis=1, keepdims=True)
        p = jnp.exp(sh - m)
        den = jnp.sum(p, axis=1, keepdims=True)
        ps.append(p.astype(BF16))
        dens.append(den)
        lses.append(m + jnp.log(den))
    return jnp.concatenate(ps, axis=0), dens, lses


def _attn_kernel(q_ref, k_ref, v_ref, o_ref, lse_ref, *, dilation, n_blocks):
    blk = ATT_BLOCK
    lane = lax.broadcasted_iota(jnp.int32, (blk, LANES), 1)
    lane_lo = lane < HEAD_DIM
    qi = lax.broadcasted_iota(jnp.int32, (blk, 2 * blk), 0)
    kj = lax.broadcasted_iota(jnp.int32, (blk, 2 * blk), 1)
    dist = qi - kj + blk
    valid2 = (dist >= 0) & (dist <= blk)
    valid1 = (lax.broadcasted_iota(jnp.int32, (blk, blk), 1)
              <= lax.broadcasted_iota(jnp.int32, (blk, blk), 0))
    stride = None if dilation == 1 else dilation

    def rows(start, n):
        return pl.ds(start, n) if stride is None else pl.ds(start, n, stride=stride)

    def do_block(start, first):
        valid = valid1 if first else valid2
        prev = start - blk * dilation

        def keys(ref, hp):
            cur = ref[hp, rows(start, blk), :].astype(BF16)
            if first:
                return cur
            return jnp.concatenate([ref[hp, rows(prev, blk), :].astype(BF16), cur], axis=0)

        scores = [_attn_scores(q_ref[hp, rows(start, blk), :].astype(BF16), keys(k_ref, hp), lane_lo)
                  for hp in range(N_SLABS)]
        soft = [_attn_softmax(s, valid) for s in scores]
        lse_rows = jnp.zeros((blk, LANES), F32)
        for hp, (p, dens, lses) in enumerate(soft):
            pv = jnp.dot(p, keys(v_ref, hp), preferred_element_type=F32)
            o_ref[hp, rows(start, blk), :] = jnp.where(lane_lo, pv[:blk] / dens[0], pv[blk:] / dens[1])
            lse_rows = jnp.where(lane == 2 * hp, lses[0], lse_rows)
            lse_rows = jnp.where(lane == 2 * hp + 1, lses[1], lse_rows)
        lse_ref[rows(start, blk), :] = lse_rows

    for r in range(dilation):
        do_block(r, True)
        if n_blocks > 1:
            def body(n, carry):
                do_block(r + n * (blk * dilation), False)
                return carry
            lax.fori_loop(1, n_blocks, body, 0)


def dilated_attention_pattern(q, k, v, bsz, seq, dilation):
    view = lambda t: t.reshape(N_SLABS, bsz, seq, LANES)
    spec = pl.BlockSpec((N_SLABS, None, seq, LANES), lambda b: (0, b, 0, 0))
    o, lse = pl.pallas_call(
        functools.partial(_attn_kernel, dilation=dilation, n_blocks=seq // dilation // ATT_BLOCK),
        grid=(bsz,),
        in_specs=[spec, spec, spec],
        out_specs=[spec, pl.BlockSpec((None, seq, LANES), lambda b: (b, 0, 0))],
        out_shape=[jax.ShapeDtypeStruct((N_SLABS, bsz, seq, LANES), F32),
                   jax.ShapeDtypeStruct((bsz, seq, LANES), F32)],
        compiler_params=_cparams("parallel"),
        name=f"dilated_attention_d{dilation}",
    )(view(q), view(k), view(v))
    return o.reshape(N_SLABS, bsz * seq, LANES), lse.reshape(bsz * seq, LANES)


def _merge_kernel(x_ref, ys_ref, o0_ref, o1_ref, o2_ref, l0_ref, l1_ref, l2_ref, hc_ref, gate_ref,
                  wglu_ref, watt_ref, wpw2_ref, wout_ref, out_ref, ys_scr):
    tm = x_ref.shape[0]

    def gate(i):
        return gate_ref[:, i * D_MODEL:(i + 1) * D_MODEL].astype(F32)

    merged = gate(2) * jnp.dot(hc_ref[...], wpw2_ref[...], preferred_element_type=F32)

    n_rows = tm // SSM_CHUNK
    strip_id = _strip_id(n_rows)
    for sl in range(N_SLABS):
        halves = [[ys_ref[:, (sl * STRIPS + gl) * CHUNK_W + jo * LANES:
                          (sl * STRIPS + gl) * CHUNK_W + (jo + 1) * LANES]
                   for gl in range(STRIPS)] for jo in range(SSM_CHUNK // STRIPS)]
        for j in range(SSM_CHUNK):
            jo, k = divmod(j, STRIPS)
            tile = _gather_strips(halves[jo], list(range(STRIPS)), [k] * STRIPS, strip_id)
            ys_scr[sl, pl.ds(j, n_rows, stride=SSM_CHUNK), :] = tile.astype(F32)
    ys = jnp.concatenate([ys_scr[sl] for sl in range(N_SLABS)], axis=1).astype(BF16)
    z = jnp.dot(ys, wglu_ref[...], preferred_element_type=F32)

    lses = [l0_ref[...], l1_ref[...], l2_ref[...]]
    m = jnp.maximum(jnp.maximum(lses[0], lses[1]), lses[2])
    es = [jnp.exp(l - m) for l in lses]
    inv = 1.0 / (es[0] + es[1] + es[2])
    ws = [e * inv for e in es]
    lane_lo = lax.broadcasted_iota(jnp.int32, (tm, LANES), 1) < HEAD_DIM
    o_refs = (o0_ref, o1_ref, o2_ref)
    pieces = []
    for hp in range(N_SLABS):
        acc = jnp.zeros((tm, LANES), F32)
        for p in range(3):
            wexp = jnp.where(lane_lo, ws[p][:, 2 * hp:2 * hp + 1], ws[p][:, 2 * hp + 1:2 * hp + 2])
            acc = acc + wexp * o_refs[p][hp]
        pieces.append(acc.astype(BF16))
    o = jnp.concatenate(pieces, axis=1)
    y_a = jnp.dot(o, watt_ref[...], preferred_element_type=F32)

    y_s = z[:, :D_MODEL] * _sigmoid(z[:, D_MODEL:])
    merged = gate(0) * y_s + gate(1) * y_a + merged
    out_ref[...] = x_ref[...] + jnp.dot(merged.astype(BF16), wout_ref[...], preferred_element_type=F32)


def merge_branches(x2, ys_chunks, os_, lses, hc, gates, w_glu, w_att, w_pw2, w_out):
    n = x2.shape[0]
    tm = TM_MERGE
    row = lambda w: pl.BlockSpec((tm, w), lambda i: (i, 0))
    slab = pl.BlockSpec((N_SLABS, tm, LANES), lambda i: (0, i, 0))
    return pl.pallas_call(
        _merge_kernel,
        grid=(n // tm,),
        in_specs=[row(D_MODEL), pl.BlockSpec((tm // SSM_CHUNK, SSM_GROUPS * CHUNK_W), lambda i: (i, 0)),
                  slab, slab, slab, row(LANES), row(LANES), row(LANES),
                  row(BRANCH_W), row(3 * D_MODEL),
                  _const_spec((BRANCH_W, 2 * D_MODEL)), _const_spec((BRANCH_W, D_MODEL)),
                  _const_spec((BRANCH_W, D_MODEL)), _const_spec((D_MODEL, D_MODEL))],
        out_specs=row(D_MODEL),
        out_shape=jax.ShapeDtypeStruct((n, D_MODEL), F32),
        scratch_shapes=[pltpu.VMEM((N_SLABS, tm, LANES), F32)],
        compiler_params=_cparams("parallel"),
        name="merge_branches",
    )(x2, ys_chunks, os_[0], os_[1], os_[2], lses[0], lses[1], lses[2], hc, gates, w_glu, w_att, w_pw2, w_out)


def _ffn_kernel(x_ref, g_ref, w1_ref, w2_ref, fg_ref, out_ref, *, final_norm):
    x = x_ref[...]
    h = _rms_norm(x, g_ref[...]).astype(BF16)
    acc = x
    for c0, c1 in zip(FF_SPLITS[:-1], FF_SPLITS[1:]):
        za = jnp.dot(h, w1_ref[:, c0:c1], preferred_element_type=F32)
        zb = jnp.dot(h, w1_ref[:, D_FF + c0:D_FF + c1], preferred_element_type=F32)
        act = (za * _sigmoid(za) * zb).astype(BF16)
        acc = acc + jnp.dot(act, w2_ref[c0:c1, :], preferred_element_type=F32)
    if final_norm:
        acc = _rms_norm(acc, fg_ref[...])
    out_ref[...] = acc


def ffn(x2, g, w1, w2, final_g, final_norm):
    n = x2.shape[0]
    tm = TM_FFN
    return pl.pallas_call(
        functools.partial(_ffn_kernel, final_norm=final_norm),
        grid=(n // tm,),
        in_specs=[pl.BlockSpec((tm, D_MODEL), lambda i: (i, 0)), _const_spec((1, D_MODEL)),
                  _const_spec((D_MODEL, 2 * D_FF)), _const_spec((D_FF, D_MODEL)), _const_spec((1, D_MODEL))],
        out_specs=pl.BlockSpec((tm, D_MODEL), lambda i: (i, 0)),
        out_shape=jax.ShapeDtypeStruct((n, D_MODEL), F32),
        compiler_params=_cparams("parallel"),
        name="ffn",
    )(x2, g.reshape(1, D_MODEL), w1, w2, final_g.reshape(1, D_MODEL))


def kernel(x, norm1_g, w_in, b_gate, ssm_lambda_re, ssm_lambda_im, ssm_log_dt, ssm_b_re, ssm_b_im, ssm_c_re,
           ssm_c_im, ssm_d, w_ssm_glu, w_att_up, conv_w, conv_b, conv_ln_g, conv_ln_b, w_conv_pw2, w_out,
           norm2_g, w_ffn_in, w_ffn_out, final_g):
    bsz, seq, _ = x.shape
    assert seq % (max(DILATIONS) * ATT_BLOCK) == 0 and seq % TM_MERGE == 0 and bsz % SSM_BATCH == 0
    x2 = x.reshape(bsz * seq, D_MODEL).astype(F32)
    all_mats = jax.vmap(ssm_matrices)(ssm_lambda_re, ssm_lambda_im, ssm_log_dt, ssm_b_re, ssm_b_im,
                                      ssm_c_re, ssm_c_im, ssm_d)
    for l in range(DEPTH):
        u, q0, q1, q2, k, v, hc, gates = in_projection(x2, norm1_g[l], w_in[l].astype(BF16), conv_w[l], conv_b[l],
                                                       conv_ln_g[l], conv_ln_b[l], b_gate[l], seq)
        ys = ssm_mixer(u, tuple(m[l] for m in all_mats), bsz, seq)
        os_, lses = [], []
        for q, dil in zip((q0, q1, q2), DILATIONS):
            o, lse = dilated_attention_pattern(q, k, v, bsz, seq, dil)
            os_.append(o)
            lses.append(lse)
        x2 = merge_branches(x2, ys, os_, lses, hc, gates, w_ssm_glu[l].astype(BF16), w_att_up[l].astype(BF16),
                            w_conv_pw2[l].astype(BF16), w_out[l].astype(BF16))
        x2 = ffn(x2, norm2_g[l], w_ffn_in[l].astype(BF16), w_ffn_out[l].astype(BF16), final_g,
                 final_norm=(l == DEPTH - 1))
    return x2.reshape(bsz, seq, D_MODEL)
```
